```python
import math
import jax, jax.numpy as jnp
from jax import lax
import numpy as np

D_MODEL = 1024
BATCH = 8
SEQ = 2048
DEPTH = 4

N_MIXERS = 2
EXPAND = 2
D_INNER = EXPAND * D_MODEL
EPS = 1e-6
GLA_HEADS = 4
GLA_KEY_DIM = D_MODEL // 2
GLA_DK = GLA_KEY_DIM // GLA_HEADS
GLA_DV = D_INNER // GLA_HEADS
GLA_GATE_RANK = 16
GLA_GATE_TAU = 16.0
GLA_CHUNK = 64
GLA_IN = 2 * GLA_KEY_DIM + 2 * D_INNER + GLA_GATE_RANK
S5_GROUP = 16
S5_GROUPS = D_INNER // S5_GROUP
S5_STATE = 64
S5_DT_MIN = 1e-3
S5_DT_MAX = 1e-1
S5_IN = 2 * D_INNER
N_GLA = (DEPTH + 1) // 2
N_S5 = DEPTH // 2

kernel_name = "hybrid_gla_s5_interleaved"


def _rmsnorm(x, g):
    x32 = x.astype(jnp.float32)
    y = x32 * lax.rsqrt(jnp.mean(x32 * x32, axis=-1, keepdims=True) + EPS) * g.astype(jnp.float32)
    return y.astype(x.dtype)


def _gla_mixer(h, w_in, w_gate_up, b_gate, head_gain, w_out):
    bsz, seqlen, _ = h.shape
    n_chunks = seqlen // GLA_CHUNK
    f32 = jnp.float32
    proj = h @ w_in
    q, k, v, z, r = jnp.split(
        proj,
        [GLA_KEY_DIM, 2 * GLA_KEY_DIM, 2 * GLA_KEY_DIM + D_INNER, 2 * GLA_KEY_DIM + 2 * D_INNER],
        axis=-1)
    log_a = jax.nn.log_sigmoid((r @ w_gate_up + b_gate).astype(f32)) / GLA_GATE_TAU

    def to_chunks(t, d):
        return t.astype(f32).reshape(bsz, n_chunks, GLA_CHUNK, GLA_HEADS, d).transpose(0, 3, 1, 2, 4)

    qc = to_chunks(q, GLA_DK) * (GLA_DK ** -0.5)
    kc = to_chunks(k, GLA_DK)
    vc = to_chunks(v, GLA_DV)
    bcum = jnp.cumsum(to_chunks(log_a, GLA_DK), axis=3)
    b_last = bcum[:, :, :, -1:, :]

    q_g = qc * jnp.exp(bcum)
    k_g = kc * jnp.exp(-bcum)
    k_end = kc * jnp.exp(b_last - bcum)

    causal = jnp.tril(jnp.ones((GLA_CHUNK, GLA_CHUNK), dtype=bool))
    att = jnp.einsum('bhncd,bhnsd->bhncs', q_g, k_g)
    att = jnp.where(causal, att, 0.0)
    o_intra = jnp.einsum('bhncs,bhnse->bhnce', att, vc)

    kv_chunk = jnp.einsum('bhnsd,bhnse->bhnde', k_end, vc)
    decay_chunk = jnp.exp(b_last[:, :, :, 0, :])

    def step(state, inp):
        dec, kv = inp
        return dec[..., None] * state + kv, state

    s0 = jnp.zeros((bsz, GLA_HEADS, GLA_DK, GLA_DV), f32)
    _, s_prev = lax.scan(step, s0, (jnp.moveaxis(decay_chunk, 2, 0), jnp.moveaxis(kv_chunk, 2, 0)))
    s_prev = jnp.moveaxis(s_prev, 0, 2)
    o = o_intra + jnp.einsum('bhncd,bhnde->bhnce', q_g, s_prev)

    o = o.transpose(0, 2, 3, 1, 4).reshape(bsz, seqlen, GLA_HEADS, GLA_DV)
    o = o * lax.rsqrt(jnp.mean(o * o, axis=-1, keepdims=True) + EPS) * head_gain.astype(f32)
    o = o.reshape(bsz, seqlen, D_INNER) * jax.nn.silu(z.astype(f32))
    return o.astype(h.dtype) @ w_out


def _s5_mixer(h, w_in, lam_re, lam_im, log_dt, b_re, b_im, c_re, c_im, d_skip, w_glu, b_glu, w_out):
    bsz, seqlen, _ = h.shape
    f32 = jnp.float32
    proj = h @ w_in
    u, z = jnp.split(proj, [D_INNER], axis=-1)
    u32 = u.astype(f32).reshape(bsz, seqlen, S5_GROUPS, S5_GROUP)

    lam = lax.complex(lam_re.astype(f32), lam_im.astype(f32))
    dt = jnp.exp(log_dt.astype(f32))[:, None]
    lam_bar = jnp.exp(lam * dt)
    b_coef = (lam_bar - 1.0) / lam
    b_cplx = lax.complex(b_re.astype(f32), b_im.astype(f32))
    b_bar = b_coef[..., None] * b_cplx
    bu = lax.complex(jnp.einsum('blgi,gpi->blgp', u32, jnp.real(b_bar)),
                     jnp.einsum('blgi,gpi->blgp', u32, jnp.imag(b_bar)))
    a = jnp.broadcast_to(lam_bar[None, None], (1, seqlen, S5_GROUPS, S5_STATE))

    def combine(left, right):
        a_l, b_l = left
        a_r, b_r = right
        return a_l * a_r, a_r * b_l + b_r

    _, states = lax.associative_scan(combine, (a, bu), axis=1)
    y = (jnp.einsum('blgp,gip->blgi', jnp.real(states), c_re.astype(f32))
         - jnp.einsum('blgp,gip->blgi', jnp.imag(states), c_im.astype(f32)))
    y = y + d_skip.astype(f32) * u32
    y = jax.nn.gelu(y.reshape(bsz, seqlen, D_INNER))
    y = y * jax.nn.sigmoid(y @ w_glu.astype(f32) + b_glu.astype(f32))
    y = y * jax.nn.silu(z.astype(f32))
    return y.astype(h.dtype) @ w_out


def setup_inputs(seed: int = 0) -> dict:
    key = jax.random.key(seed)
    ks = jax.random.split(key, 24)
    nrm = jax.random.normal
    f32 = jnp.float32
    x = nrm(ks[0], (BATCH, SEQ, D_MODEL), f32)
    gla_norm = 1.0 + 0.01 * nrm(ks[1], (N_GLA, D_MODEL), f32)
    gla_w_in = nrm(ks[2], (N_GLA, D_MODEL, GLA_IN), f32) * D_MODEL ** -0.5
    gla_w_gate_up = nrm(ks[3], (N_GLA, GLA_GATE_RANK, GLA_KEY_DIM), f32) * GLA_GATE_RANK ** -0.5
    gla_b_gate = 0.1 * nrm(ks[4], (N_GLA, GLA_KEY_DIM), f32)
    gla_head_gain = 1.0 + 0.01 * nrm(ks[5], (N_GLA, GLA_DV), f32)
    gla_w_out = nrm(ks[6], (N_GLA, D_INNER, D_MODEL), f32) * D_INNER ** -0.5
    s5_norm = 1.0 + 0.01 * nrm(ks[7], (N_S5, D_MODEL), f32)
    s5_w_in = nrm(ks[8], (N_S5, D_MODEL, S5_IN), f32) * D_MODEL ** -0.5
    s5_lam_re = -0.5 + 1e-3 * nrm(ks[9], (N_S5, S5_GROUPS, S5_STATE), f32)
    s5_lam_im = (math.pi * jnp.arange(S5_STATE, dtype=f32))[None, None, :] \
        + 1e-3 * nrm(ks[10], (N_S5, S5_GROUPS, S5_STATE), f32)
    s5_log_dt = jax.random.uniform(ks[11], (N_S5, S5_GROUPS), f32,
                                   minval=math.log(S5_DT_MIN), maxval=math.log(S5_DT_MAX))
    s5_b_re = nrm(ks[12], (N_S5, S5_GROUPS, S5_STATE, S5_GROUP), f32) * (2 * S5_GROUP) ** -0.5
    s5_b_im = nrm(ks[13], (N_S5, S5_GROUPS, S5_STATE, S5_GROUP), f32) * (2 * S5_GROUP) ** -0.5
    s5_c_re = nrm(ks[14], (N_S5, S5_GROUPS, S5_GROUP, S5_STATE), f32) * S5_STATE ** -0.5
    s5_c_im = nrm(ks[15], (N_S5, S5_GROUPS, S5_GROUP, S5_STATE), f32) * S5_STATE ** -0.5
    s5_d = nrm(ks[16], (N_S5, S5_GROUPS, S5_GROUP), f32)
    s5_w_glu = nrm(ks[17], (N_S5, D_INNER, D_INNER), f32) * D_INNER ** -0.5
    s5_b_glu = 0.01 * nrm(ks[18], (N_S5, D_INNER), f32)
    s5_w_out = nrm(ks[19], (N_S5, D_INNER, D_MODEL), f32) * D_INNER ** -0.5
    final_norm = 1.0 + 0.01 * nrm(ks[20], (D_MODEL,), f32)
    return {
        "x": x,
        "gla_norm": gla_norm, "gla_w_in": gla_w_in, "gla_w_gate_up": gla_w_gate_up,
        "gla_b_gate": gla_b_gate, "gla_head_gain": gla_head_gain, "gla_w_out": gla_w_out,
        "s5_norm": s5_norm, "s5_w_in": s5_w_in, "s5_lam_re": s5_lam_re, "s5_lam_im": s5_lam_im,
        "s5_log_dt": s5_log_dt, "s5_b_re": s5_b_re, "s5_b_im": s5_b_im,
        "s5_c_re": s5_c_re, "s5_c_im": s5_c_im, "s5_d": s5_d,
        "s5_w_glu": s5_w_glu, "s5_b_glu": s5_b_glu, "s5_w_out": s5_w_out,
        "final_norm": final_norm,
    }


def reference(x, gla_norm, gla_w_in, gla_w_gate_up, gla_b_gate, gla_head_gain, gla_w_out,
              s5_norm, s5_w_in, s5_lam_re, s5_lam_im, s5_log_dt, s5_b_re, s5_b_im,
              s5_c_re, s5_c_im, s5_d, s5_w_glu, s5_b_glu, s5_w_out, final_norm):
    for i in range(DEPTH):
        j = i // N_MIXERS
        if i % N_MIXERS == 0:
            h = _rmsnorm(x, gla_norm[j])
            out = _gla_mixer(h, gla_w_in[j], gla_w_gate_up[j], gla_b_gate[j],
                             gla_head_gain[j], gla_w_out[j])
        else:
            h = _rmsnorm(x, s5_norm[j])
            out = _s5_mixer(h, s5_w_in[j], s5_lam_re[j], s5_lam_im[j], s5_log_dt[j],
                            s5_b_re[j], s5_b_im[j], s5_c_re[j], s5_c_im[j], s5_d[j],
                            s5_w_glu[j], s5_b_glu[j], s5_w_out[j])
        x = x + out.astype(x.dtype)
    return _rmsnorm(x, final_norm)
```

```python
import functools
import math

import jax
import jax.numpy as jnp
from jax import lax
from jax.experimental import pallas as pl
from jax.experimental.pallas import tpu as pltpu

F32 = jnp.float32
BF16 = jnp.bfloat16

EPS = 1e-6
GLA_HEADS = 4
GLA_GATE_TAU = 16.0
GLA_CHUNK = 64
S5_GROUP = 16
S5_STATE = 64

LANES = 128
SUBLANES = 8
VMEM_LIMIT_BYTES = 56 * 1024 * 1024

ROW_TILE = 256
GLA_TILE = 256
SSM_TILE = 128


def _params(*sem):
    return pltpu.CompilerParams(dimension_semantics=sem, vmem_limit_bytes=VMEM_LIMIT_BYTES)


def _mm(a, b):
    return jnp.dot(a, b, preferred_element_type=F32)


def _mm_nt(a, b):
    return lax.dot_general(a, b, (((1,), (1,)), ((), ())), preferred_element_type=F32)


def _mm_tn(a, b):
    return lax.dot_general(a, b, (((0,), (0,)), ((), ())), preferred_element_type=F32)


def _rms(x, g):
    return x * lax.rsqrt(jnp.mean(x * x, axis=-1, keepdims=True) + EPS) * g


def _norm_proj_kernel(x_ref, g_ref, w_ref, o_ref):
    h = _rms(x_ref[...], g_ref[...])
    o_ref[...] = _mm(h.astype(BF16), w_ref[...])


def _norm_proj(x2d, g, w_bf16):
    t, d = x2d.shape
    n = w_bf16.shape[1]
    return pl.pallas_call(
        _norm_proj_kernel,
        grid=(t // ROW_TILE,),
        in_specs=[pl.BlockSpec((ROW_TILE, d), lambda i: (i, 0)),
                  pl.BlockSpec((1, d), lambda i: (0, 0)),
                  pl.BlockSpec((d, n), lambda i: (0, 0))],
        out_specs=pl.BlockSpec((ROW_TILE, n), lambda i: (i, 0)),
        out_shape=jax.ShapeDtypeStruct((t, n), F32),
        compiler_params=_params("parallel"),
        name="norm_proj",
    )(x2d, g.reshape(1, d), w_bf16)


def _proj_res_kernel(a_ref, w_ref, x_ref, o_ref):
    o_ref[...] = x_ref[...] + _mm(a_ref[...], w_ref[...])


def _proj_res(a_bf16, w_bf16, x2d):
    t, k = a_bf16.shape
    d = w_bf16.shape[1]
    return pl.pallas_call(
        _proj_res_kernel,
        grid=(t // ROW_TILE,),
        in_specs=[pl.BlockSpec((ROW_TILE, k), lambda i: (i, 0)),
                  pl.BlockSpec((k, d), lambda i: (0, 0)),
                  pl.BlockSpec((ROW_TILE, d), lambda i: (i, 0))],
        out_specs=pl.BlockSpec((ROW_TILE, d), lambda i: (i, 0)),
        out_shape=jax.ShapeDtypeStruct((t, d), F32),
        compiler_params=_params("parallel"),
        name="proj_res",
    )(a_bf16, w_bf16, x2d)


def _final_norm_kernel(x_ref, g_ref, o_ref):
    o_ref[...] = _rms(x_ref[...], g_ref[...])


def _final_norm(x2d, g):
    t, d = x2d.shape
    return pl.pallas_call(
        _final_norm_kernel,
        grid=(t // ROW_TILE,),
        in_specs=[pl.BlockSpec((ROW_TILE, d), lambda i: (i, 0)),
                  pl.BlockSpec((1, d), lambda i: (0, 0))],
        out_specs=pl.BlockSpec((ROW_TILE, d), lambda i: (i, 0)),
        out_shape=jax.ShapeDtypeStruct((t, d), F32),
        compiler_params=_params("parallel"),
        name="final_norm",
    )(x2d, g.reshape(1, d))


def _split3_bf16(x):
    hi = x.astype(BF16)
    r1 = x - hi.astype(F32)
    mid = r1.astype(BF16)
    lo = (r1 - mid.astype(F32)).astype(BF16)
    return hi, mid, lo


def _gla_kernel(q_ref, k_ref, v_ref, z_ref, r_ref, wup_ref, bg_ref, gain_ref,
                o_ref, s_ref, *, dk):
    tl = q_ref.shape[0]
    c = GLA_CHUNK

    @pl.when(pl.program_id(2) == 0)
    def _():
        s_ref[...] = jnp.zeros_like(s_ref)

    gate = _mm(r_ref[...].astype(BF16), wup_ref[...]) + bg_ref[...]
    log_a = jax.nn.log_sigmoid(gate) / GLA_GATE_TAU

    row = lax.broadcasted_iota(jnp.int32, (tl, tl), 0)
    col = lax.broadcasted_iota(jnp.int32, (tl, tl), 1)
    ltri = jnp.where((row // c == col // c) & (col <= row), 1.0, 0.0).astype(BF16)
    hi, mid, lo = _split3_bf16(log_a)
    bcum = _mm(ltri, hi) + _mm(ltri, mid) + _mm(ltri, lo)

    crow = lax.broadcasted_iota(jnp.int32, (c, c), 0)
    ccol = lax.broadcasted_iota(jnp.int32, (c, c), 1)
    causal = ccol <= crow
    drow = lax.broadcasted_iota(jnp.int32, (dk, dk), 0)
    dcol = lax.broadcasted_iota(jnp.int32, (dk, dk), 1)
    eye = drow == dcol
    scale = dk ** -0.5

    for ci in range(tl // c):
        sl = slice(ci * c, (ci + 1) * c)
        bc = bcum[sl]
        bl = bc[c - 1:c, :]
        q = q_ref[sl, :]
        k = k_ref[sl, :]
        qg = (q * scale) * jnp.exp(bc)
        kg = k * jnp.exp(-bc)
        ke = k * jnp.exp(bl - bc)
        vb = v_ref[sl, :].astype(BF16)
        qgb = qg.astype(BF16)

        att = jnp.where(causal, _mm_nt(qgb, kg.astype(BF16)), 0.0)
        s_prev = s_ref[...]
        o = _mm(att.astype(BF16), vb) + _mm(qgb, s_prev.astype(BF16))

        dec = jnp.exp(bl)
        dec_col = jnp.sum(jnp.where(eye, jnp.broadcast_to(dec, (dk, dk)), 0.0),
                          axis=1, keepdims=True)
        s_ref[...] = dec_col * s_prev + _mm_tn(ke.astype(BF16), vb)

        on = _rms(o, gain_ref[...])
        o_ref[sl, :] = (on * jax.nn.silu(z_ref[sl, :])).astype(o_ref.dtype)


def _gla_core(proj, w_up_pad, b_gate, head_gain, bsz, seqlen, dk, dv):
    t = bsz * seqlen
    h = GLA_HEADS
    nt = seqlen // GLA_TILE
    kd = h * dk
    k_blk = kd // dk
    v_blk = (2 * kd) // dv
    z_blk = (2 * kd + h * dv) // dv
    r_blk = (2 * kd + 2 * h * dv) // LANES
    rows = lambda b, hh, i: b * nt + i
    return pl.pallas_call(
        functools.partial(_gla_kernel, dk=dk),
        grid=(bsz, h, nt),
        in_specs=[
            pl.BlockSpec((GLA_TILE, dk), lambda b, hh, i: (rows(b, hh, i), hh)),
            pl.BlockSpec((GLA_TILE, dk), lambda b, hh, i: (rows(b, hh, i), k_blk + hh)),
            pl.BlockSpec((GLA_TILE, dv), lambda b, hh, i: (rows(b, hh, i), v_blk + hh)),
            pl.BlockSpec((GLA_TILE, dv), lambda b, hh, i: (rows(b, hh, i), z_blk + hh)),
            pl.BlockSpec((GLA_TILE, LANES), lambda b, hh, i: (rows(b, hh, i), r_blk)),
            pl.BlockSpec((LANES, dk), lambda b, hh, i: (0, hh)),
            pl.BlockSpec((1, dk), lambda b, hh, i: (0, hh)),
            pl.BlockSpec((1, dv), lambda b, hh, i: (0, 0)),
        ],
        out_specs=pl.BlockSpec((GLA_TILE, dv), lambda b, hh, i: (rows(b, hh, i), hh)),
        out_shape=jax.ShapeDtypeStruct((t, h * dv), BF16),
        scratch_shapes=[pltpu.VMEM((dk, dv), F32)],
        compiler_params=_params("parallel", "parallel", "arbitrary"),
        name="gla_core",
    )(proj, proj, proj, proj, proj, w_up_pad, b_gate.reshape(1, kd), head_gain.reshape(1, dv))


def _gla_layer(x2d, bsz, seqlen, norm_g, w_in, w_gate_up, b_gate, head_gain, w_out):
    d = x2d.shape[1]
    rank, kd = w_gate_up.shape
    d_inner = w_out.shape[0]
    dk = kd // GLA_HEADS
    dv = d_inner // GLA_HEADS
    n_main = 2 * kd + 2 * d_inner
    w_in_pad = jnp.concatenate(
        [w_in[:, :n_main], w_in[:, n_main:], jnp.zeros((d, LANES - rank), w_in.dtype)], axis=1)
    w_up_pad = jnp.concatenate(
        [w_gate_up, jnp.zeros((LANES - rank, kd), w_gate_up.dtype)], axis=0)
    proj = _norm_proj(x2d, norm_g, w_in_pad.astype(BF16))
    o = _gla_core(proj, w_up_pad.astype(BF16), b_gate, head_gain, bsz, seqlen, dk, dv)
    return _proj_res(o, w_out.astype(BF16), x2d)


def _ssm_kernel(u_ref, bm_ref, cm_ref, are_ref, aim_ref, d_ref, y_ref,
                ulb_ref, bu_ref, st_ref, ylb_ref):
    nb, tl, _ = u_ref.shape
    ns = st_ref.shape[1] // 2

    @pl.when(pl.program_id(1) == 0)
    def _():
        st_ref[...] = jnp.zeros_like(st_ref)

    for b in range(nb):
        ulb_ref[pl.ds(b, tl, stride=nb), :] = u_ref[b]
    ulb = ulb_ref[...]
    bu_ref[...] = _mm(ulb.astype(BF16), bm_ref[0])

    a_re = jnp.broadcast_to(are_ref[0], (nb, ns))
    a_im = jnp.broadcast_to(aim_ref[0], (nb, ns))

    def step(l, carry):
        s_re, s_im = carry
        rows = pl.ds(pl.multiple_of(l * nb, nb), nb)
        n_re = a_re * s_re - a_im * s_im + bu_ref[rows, 0:ns]
        n_im = a_re * s_im + a_im * s_re + bu_ref[rows, ns:2 * ns]
        bu_ref[rows, 0:ns] = n_re
        bu_ref[rows, ns:2 * ns] = n_im
        return n_re, n_im

    s_re, s_im = lax.fori_loop(0, tl, step, (st_ref[:, 0:ns], st_ref[:, ns:2 * ns]), unroll=8)
    st_ref[:, 0:ns] = s_re
    st_ref[:, ns:2 * ns] = s_im

    y = _mm(bu_ref[...].astype(BF16), cm_ref[0]) + d_ref[...] * ulb
    ylb_ref[...] = jax.nn.gelu(y)
    for b in range(nb):
        y_ref[b] = ylb_ref[pl.ds(b, tl, stride=nb), :]


def _ssm_core(proj3, bmat, cmat, a_re, a_im, d_row):
    bsz, seqlen, _ = proj3.shape
    nblk, _, two_ns = bmat.shape
    d_inner = nblk * LANES
    tl = SSM_TILE
    return pl.pallas_call(
        _ssm_kernel,
        grid=(nblk, seqlen // tl),
        in_specs=[
            pl.BlockSpec((bsz, tl, LANES), lambda j, i: (0, i, j)),
            pl.BlockSpec((1, LANES, two_ns), lambda j, i: (j, 0, 0)),
            pl.BlockSpec((1, two_ns, LANES), lambda j, i: (j, 0, 0)),
            pl.BlockSpec((1, 1, two_ns // 2), lambda j, i: (j, 0, 0)),
            pl.BlockSpec((1, 1, two_ns // 2), lambda j, i: (j, 0, 0)),
            pl.BlockSpec((1, LANES), lambda j, i: (0, j)),
        ],
        out_specs=pl.BlockSpec((bsz, tl, LANES), lambda j, i: (0, i, j)),
        out_shape=jax.ShapeDtypeStruct((bsz, seqlen, d_inner), F32),
        scratch_shapes=[pltpu.VMEM((tl * bsz, LANES), F32),
                        pltpu.VMEM((tl * bsz, two_ns), F32),
                        pltpu.VMEM((bsz, two_ns), F32),
                        pltpu.VMEM((tl * bsz, LANES), F32)],
        compiler_params=_params("parallel", "arbitrary"),
        name="ssm_core",
    )(proj3, bmat, cmat, a_re, a_im, d_row)


def _glu_out_kernel(y_ref, z_ref, wg_ref, bg_ref, wo_ref, x_ref, o_ref):
    y = y_ref[...]
    y = y * jax.nn.sigmoid(_mm(y.astype(BF16), wg_ref[...]) + bg_ref[...])
    y = y * jax.nn.silu(z_ref[...])
    o_ref[...] = x_ref[...] + _mm(y.astype(BF16), wo_ref[...])


def _glu_out(y2d, proj, w_glu, b_glu, w_out, x2d):
    t, di = y2d.shape
    d = x2d.shape[1]
    return pl.pallas_call(
        _glu_out_kernel,
        grid=(t // ROW_TILE,),
        in_specs=[pl.BlockSpec((ROW_TILE, di), lambda i: (i, 0)),
                  pl.BlockSpec((ROW_TILE, di), lambda i: (i, 1)),
                  pl.BlockSpec((di, di), lambda i: (0, 0)),
                  pl.BlockSpec((1, di), lambda i: (0, 0)),
                  pl.BlockSpec((di, d), lambda i: (0, 0)),
                  pl.BlockSpec((ROW_TILE, d), lambda i: (i, 0))],
        out_specs=pl.BlockSpec((ROW_TILE, d), lambda i: (i, 0)),
        out_shape=jax.ShapeDtypeStruct((t, d), F32),
        compiler_params=_params("parallel"),
        name="glu_out",
    )(y2d, proj, w_glu, b_glu.reshape(1, di), w_out, x2d)


def _s5_discretize(lam_re, lam_im, log_dt, b_re, b_im, c_re, c_im):
    g, p = lam_re.shape
    i = b_re.shape[-1]
    gl = LANES // i
    nblk = g // gl
    ns = gl * p
    dt = jnp.exp(log_dt)[:, None]
    mag = jnp.exp(lam_re * dt)
    lb_re = mag * jnp.cos(lam_im * dt)
    lb_im = mag * jnp.sin(lam_im * dt)
    den = lam_re * lam_re + lam_im * lam_im
    bc_re = ((lb_re - 1.0) * lam_re + lb_im * lam_im) / den
    bc_im = (lb_im * lam_re - (lb_re - 1.0) * lam_im) / den
    bb_re = bc_re[..., None] * b_re - bc_im[..., None] * b_im
    bb_im = bc_re[..., None] * b_im + bc_im[..., None] * b_re
    eye = jnp.eye(gl, dtype=F32)

    def pack_b(m):
        m = m.reshape(nblk, gl, p, i)
        return jnp.einsum('jgpi,gh->jgihp', m, eye).reshape(nblk, gl * i, ns)

    def pack_c(m):
        m = m.reshape(nblk, gl, i, p)
        return jnp.einsum('jgip,gh->jhpgi', m, eye).reshape(nblk, ns, gl * i)

    bmat = jnp.concatenate([pack_b(bb_re), pack_b(bb_im)], axis=2)
    cmat = jnp.concatenate([pack_c(c_re), pack_c(-c_im)], axis=1)
    return (bmat.astype(BF16), cmat.astype(BF16),
            lb_re.reshape(nblk, 1, ns), lb_im.reshape(nblk, 1, ns))


def _s5_layer(x2d, bsz, seqlen, norm_g, w_in, lam_re, lam_im, log_dt, b_re, b_im,
              c_re, c_im, d_skip, w_glu, b_glu, w_out):
    d_inner = w_out.shape[0]
    proj = _norm_proj(x2d, norm_g, w_in.astype(BF16))
    bmat, cmat, a_re, a_im = _s5_discretize(lam_re, lam_im, log_dt, b_re, b_im, c_re, c_im)
    y = _ssm_core(proj.reshape(bsz, seqlen, 2 * d_inner), bmat, cmat, a_re, a_im,
                  d_skip.reshape(1, d_inner))
    return _glu_out(y.reshape(bsz * seqlen, d_inner), proj, w_glu.astype(BF16), b_glu,
                    w_out.astype(BF16), x2d)


def kernel(x, gla_norm, gla_w_in, gla_w_gate_up, gla_b_gate, gla_head_gain, gla_w_out,
           s5_norm, s5_w_in, s5_lam_re, s5_lam_im, s5_log_dt, s5_b_re, s5_b_im,
           s5_c_re, s5_c_im, s5_d, s5_w_glu, s5_b_glu, s5_w_out, final_norm):
    bsz, seqlen, d = x.shape
    n_gla, n_s5 = gla_w_in.shape[0], s5_w_in.shape[0]
    x2d = x.reshape(bsz * seqlen, d)
    for i in range(n_gla + n_s5):
        j = i // 2
        if i % 2 == 0:
            x2d = _gla_layer(x2d, bsz, seqlen, gla_norm[j], gla_w_in[j], gla_w_gate_up[j],
                             gla_b_gate[j], gla_head_gain[j], gla_w_out[j])
        else:
            x2d = _s5_layer(x2d, bsz, seqlen, s5_norm[j], s5_w_in[j], s5_lam_re[j], s5_lam_im[j],
                            s5_log_dt[j], s5_b_re[j], s5_b_im[j], s5_c_re[j], s5_c_im[j], s5_d[j],
                            s5_w_glu[j], s5_b_glu[j], s5_w_out[j])
    return _final_norm(x2d, final_norm).reshape(bsz, seqlen, d)
```

```python
import functools
import math

import jax
import jax.numpy as jnp
from jax import lax
from jax.experimental import pallas as pl
from jax.experimental.pallas import tpu as pltpu

F32 = jnp.float32
BF16 = jnp.bfloat16

EPS = 1e-6
GLA_HEADS = 4
GLA_GATE_TAU = 16.0
GLA_CHUNK = 64
S5_GROUP = 16
S5_STATE = 64

LANES = 128
SUBLANES = 8
VMEM_LIMIT_BYTES = 56 * 1024 * 1024

ROW_TILE = 256
GLA_TILE = 256
SSM_TILE = 256
SSM_SUB_PAIRS = 32


def _params(*sem):
    return pltpu.CompilerParams(dimension_semantics=sem, vmem_limit_bytes=VMEM_LIMIT_BYTES)


def _mm(a, b):
    return jnp.dot(a, b, preferred_element_type=F32)


def _mm_nt(a, b):
    return lax.dot_general(a, b, (((1,), (1,)), ((), ())), preferred_element_type=F32)


def _mm_tn(a, b):
    return lax.dot_general(a, b, (((0,), (0,)), ((), ())), preferred_element_type=F32)


def _rms(x, g):
    return x * lax.rsqrt(jnp.mean(x * x, axis=-1, keepdims=True) + EPS) * g


def _norm_proj_kernel(x_ref, g_ref, w_ref, o_ref):
    h = _rms(x_ref[...], g_ref[...])
    o_ref[...] = _mm(h.astype(BF16), w_ref[...])


def _norm_proj(x2d, g, w_bf16):
    t, d = x2d.shape
    n = w_bf16.shape[1]
    return pl.pallas_call(
        _norm_proj_kernel,
        grid=(t // ROW_TILE,),
        in_specs=[pl.BlockSpec((ROW_TILE, d), lambda i: (i, 0)),
                  pl.BlockSpec((1, d), lambda i: (0, 0)),
                  pl.BlockSpec((d, n), lambda i: (0, 0))],
        out_specs=pl.BlockSpec((ROW_TILE, n), lambda i: (i, 0)),
        out_shape=jax.ShapeDtypeStruct((t, n), F32),
        compiler_params=_params("parallel"),
        name="norm_proj",
    )(x2d, g.reshape(1, d), w_bf16)


def _proj_res_kernel(a_ref, w_ref, x_ref, o_ref):
    o_ref[...] = x_ref[...] + _mm(a_ref[...], w_ref[...])


def _proj_res(a_bf16, w_bf16, x2d):
    t, k = a_bf16.shape
    d = w_bf16.shape[1]
    return pl.pallas_call(
        _proj_res_kernel,
        grid=(t // ROW_TILE,),
        in_specs=[pl.BlockSpec((ROW_TILE, k), lambda i: (i, 0)),
                  pl.BlockSpec((k, d), lambda i: (0, 0)),
                  pl.BlockSpec((ROW_TILE, d), lambda i: (i, 0))],
        out_specs=pl.BlockSpec((ROW_TILE, d), lambda i: (i, 0)),
        out_shape=jax.ShapeDtypeStruct((t, d), F32),
        compiler_params=_params("parallel"),
        name="proj_res",
    )(a_bf16, w_bf16, x2d)


def _final_norm_kernel(x_ref, g_ref, o_ref):
    o_ref[...] = _rms(x_ref[...], g_ref[...])


def _final_norm(x2d, g):
    t, d = x2d.shape
    return pl.pallas_call(
        _final_norm_kernel,
        grid=(t // ROW_TILE,),
        in_specs=[pl.BlockSpec((ROW_TILE, d), lambda i: (i, 0)),
                  pl.BlockSpec((1, d), lambda i: (0, 0))],
        out_specs=pl.BlockSpec((ROW_TILE, d), lambda i: (i, 0)),
        out_shape=jax.ShapeDtypeStruct((t, d), F32),
        compiler_params=_params("parallel"),
        name="final_norm",
    )(x2d, g.reshape(1, d))


def _split3_bf16(x):
    hi = x.astype(BF16)
    r1 = x - hi.astype(F32)
    mid = r1.astype(BF16)
    lo = (r1 - mid.astype(F32)).astype(BF16)
    return hi, mid, lo


def _gla_kernel(q_ref, k_ref, v_ref, z_ref, r_ref, wup_ref, bg_ref, gain_ref,
                o_ref, s_ref, *, dk):
    tl = q_ref.shape[0]
    c = GLA_CHUNK

    @pl.when(pl.program_id(2) == 0)
    def _():
        s_ref[...] = jnp.zeros_like(s_ref)

    gate = _mm(r_ref[...].astype(BF16), wup_ref[...]) + bg_ref[...]
    log_a = jax.nn.log_sigmoid(gate) / GLA_GATE_TAU

    row = lax.broadcasted_iota(jnp.int32, (tl, tl), 0)
    col = lax.broadcasted_iota(jnp.int32, (tl, tl), 1)
    ltri = jnp.where((row // c == col // c) & (col <= row), 1.0, 0.0).astype(BF16)
    hi, mid, lo = _split3_bf16(log_a)
    bcum = _mm(ltri, hi) + _mm(ltri, mid) + _mm(ltri, lo)

    crow = lax.broadcasted_iota(jnp.int32, (c, c), 0)
    ccol = lax.broadcasted_iota(jnp.int32, (c, c), 1)
    causal = ccol <= crow
    drow = lax.broadcasted_iota(jnp.int32, (dk, dk), 0)
    dcol = lax.broadcasted_iota(jnp.int32, (dk, dk), 1)
    eye = drow == dcol
    scale = dk ** -0.5

    for ci in range(tl // c):
        sl = slice(ci * c, (ci + 1) * c)
        bc = bcum[sl]
        bl = bc[c - 1:c, :]
        q = q_ref[sl, :]
        k = k_ref[sl, :]
        qg = (q * scale) * jnp.exp(bc)
        kg = k * jnp.exp(-bc)
        ke = k * jnp.exp(bl - bc)
        vb = v_ref[sl, :].astype(BF16)
        qgb = qg.astype(BF16)

        att = jnp.where(causal, _mm_nt(qgb, kg.astype(BF16)), 0.0)
        s_prev = s_ref[...]
        o = _mm(att.astype(BF16), vb) + _mm(qgb, s_prev.astype(BF16))

        dec = jnp.exp(bl)
        dec_col = jnp.sum(jnp.where(eye, jnp.broadcast_to(dec, (dk, dk)), 0.0),
                          axis=1, keepdims=True)
        s_ref[...] = dec_col * s_prev + _mm_tn(ke.astype(BF16), vb)

        on = _rms(o, gain_ref[...])
        o_ref[sl, :] = (on * jax.nn.silu(z_ref[sl, :])).astype(o_ref.dtype)


def _gla_core(proj, w_up_pad, b_gate, head_gain, bsz, seqlen, dk, dv):
    t = bsz * seqlen
    h = GLA_HEADS
    nt = seqlen // GLA_TILE
    kd = h * dk
    k_blk = kd // dk
    v_blk = (2 * kd) // dv
    z_blk = (2 * kd + h * dv) // dv
    r_blk = (2 * kd + 2 * h * dv) // LANES
    rows = lambda b, hh, i: b * nt + i
    return pl.pallas_call(
        functools.partial(_gla_kernel, dk=dk),
        grid=(bsz, h, nt),
        in_specs=[
            pl.BlockSpec((GLA_TILE, dk), lambda b, hh, i: (rows(b, hh, i), hh)),
            pl.BlockSpec((GLA_TILE, dk), lambda b, hh, i: (rows(b, hh, i), k_blk + hh)),
            pl.BlockSpec((GLA_TILE, dv), lambda b, hh, i: (rows(b, hh, i), v_blk + hh)),
            pl.BlockSpec((GLA_TILE, dv), lambda b, hh, i: (rows(b, hh, i), z_blk + hh)),
            pl.BlockSpec((GLA_TILE, LANES), lambda b, hh, i: (rows(b, hh, i), r_blk)),
            pl.BlockSpec((LANES, dk), lambda b, hh, i: (0, hh)),
            pl.BlockSpec((1, dk), lambda b, hh, i: (0, hh)),
            pl.BlockSpec((1, dv), lambda b, hh, i: (0, 0)),
        ],
        out_specs=pl.BlockSpec((GLA_TILE, dv), lambda b, hh, i: (rows(b, hh, i), hh)),
        out_shape=jax.ShapeDtypeStruct((t, h * dv), BF16),
        scratch_shapes=[pltpu.VMEM((dk, dv), F32)],
        compiler_params=_params("parallel", "parallel", "arbitrary"),
        name="gla_core",
    )(proj, proj, proj, proj, proj, w_up_pad, b_gate.reshape(1, kd), head_gain.reshape(1, dv))


def _gla_layer(x2d, bsz, seqlen, norm_g, w_in, w_gate_up, b_gate, head_gain, w_out):
    d = x2d.shape[1]
    rank, kd = w_gate_up.shape
    d_inner = w_out.shape[0]
    dk = kd // GLA_HEADS
    dv = d_inner // GLA_HEADS
    n_main = 2 * kd + 2 * d_inner
    w_in_pad = jnp.concatenate(
        [w_in[:, :n_main], w_in[:, n_main:], jnp.zeros((d, LANES - rank), w_in.dtype)], axis=1)
    w_up_pad = jnp.concatenate(
        [w_gate_up, jnp.zeros((LANES - rank, kd), w_gate_up.dtype)], axis=0)
    proj = _norm_proj(x2d, norm_g, w_in_pad.astype(BF16))
    o = _gla_core(proj, w_up_pad.astype(BF16), b_gate, head_gain, bsz, seqlen, dk, dv)
    return _proj_res(o, w_out.astype(BF16), x2d)


def _ssm_kernel(u_ref, bm_ref, cm_ref, cb_ref, a2re_ref, a2im_ref, d_ref, y_ref,
                ue_ref, uo_ref, u2_ref, w_ref, tb_ref, st_ref, zsh_ref, ye_ref, yo_ref):
    nb, tl, lanes = u_ref.shape
    npair = tl // 2
    rows = npair * nb
    ns = st_ref.shape[1] // 2
    sub = SSM_SUB_PAIRS
    sub_rows = sub * nb

    @pl.when(pl.program_id(1) == 0)
    def _():
        st_ref[...] = jnp.zeros_like(st_ref)
        zsh_ref[0:nb, :] = jnp.zeros((nb, lanes), F32)

    for b in range(nb):
        ue_ref[pl.ds(b, npair, stride=nb), :] = u_ref[b, pl.ds(0, npair, stride=2), :]
        uo_ref[pl.ds(b, npair, stride=nb), :] = u_ref[b, pl.ds(1, npair, stride=2), :]

    u2_ref[...] = jnp.concatenate([ue_ref[...], uo_ref[...]], axis=1).astype(BF16)

    a_re = jnp.broadcast_to(a2re_ref[0], (nb, ns))
    a_im = jnp.broadcast_to(a2im_ref[0], (nb, ns))
    d = d_ref[...]
    s_re = st_ref[:, 0:ns]
    s_im = st_ref[:, ns:2 * ns]

    for k in range(npair // sub):
        blk = slice(k * sub_rows, (k + 1) * sub_rows)
        w_ref[blk, :] = _mm(u2_ref[blk, :], bm_ref[0])
        for m in range(0, sub, 2):
            r0 = k * sub_rows + m * nb
            m_re = a_re * s_re - a_im * s_im + w_ref[r0:r0 + nb, 0:ns]
            m_im = a_re * s_im + a_im * s_re + w_ref[r0:r0 + nb, ns:2 * ns]
            s_re = a_re * m_re - a_im * m_im + w_ref[r0 + nb:r0 + 2 * nb, 0:ns]
            s_im = a_re * m_im + a_im * m_re + w_ref[r0 + nb:r0 + 2 * nb, ns:2 * ns]
            tb_ref[r0:r0 + 2 * nb, 0:ns] = jnp.concatenate([m_re, s_re], axis=0).astype(BF16)
            tb_ref[r0:r0 + 2 * nb, ns:2 * ns] = jnp.concatenate([m_im, s_im], axis=0).astype(BF16)
        z = _mm(tb_ref[blk, :], cm_ref[0])
        zsh_ref[nb + k * sub_rows:nb + (k + 1) * sub_rows, :] = z[:, 0:lanes]
        ye_ref[blk, :] = (zsh_ref[blk, :] + _mm(u2_ref[blk, 0:lanes], cb_ref[0])
                          + d * ue_ref[blk, :])
        yo_ref[blk, :] = z[:, lanes:2 * lanes] + d * uo_ref[blk, :]

    st_ref[:, 0:ns] = s_re
    st_ref[:, ns:2 * ns] = s_im
    zsh_ref[0:nb, :] = zsh_ref[rows:rows + nb, :]
    for b in range(nb):
        y_ref[b, pl.ds(0, npair, stride=2), :] = ye_ref[pl.ds(b, npair, stride=nb), :]
        y_ref[b, pl.ds(1, npair, stride=2), :] = yo_ref[pl.ds(b, npair, stride=nb), :]


def _ssm_core(proj3, bmat, cmat, cbmat, a2_re, a2_im, d_row):
    bsz, seqlen, _ = proj3.shape
    nblk, _, two_ns = bmat.shape
    d_inner = nblk * LANES
    tl = SSM_TILE
    rows = (tl // 2) * bsz
    return pl.pallas_call(
        _ssm_kernel,
        grid=(nblk, seqlen // tl),
        in_specs=[
            pl.BlockSpec((bsz, tl, LANES), lambda j, i: (0, i, j)),
            pl.BlockSpec((1, 2 * LANES, two_ns), lambda j, i: (j, 0, 0)),
            pl.BlockSpec((1, two_ns, 2 * LANES), lambda j, i: (j, 0, 0)),
            pl.BlockSpec((1, LANES, LANES), lambda j, i: (j, 0, 0)),
            pl.BlockSpec((1, 1, two_ns // 2), lambda j, i: (j, 0, 0)),
            pl.BlockSpec((1, 1, two_ns // 2), lambda j, i: (j, 0, 0)),
            pl.BlockSpec((1, LANES), lambda j, i: (0, j)),
        ],
        out_specs=pl.BlockSpec((bsz, tl, LANES), lambda j, i: (0, i, j)),
        out_shape=jax.ShapeDtypeStruct((bsz, seqlen, d_inner), F32),
        scratch_shapes=[pltpu.VMEM((rows, LANES), F32),
                        pltpu.VMEM((rows, LANES), F32),
                        pltpu.VMEM((rows, 2 * LANES), BF16),
                        pltpu.VMEM((rows, two_ns), F32),
                        pltpu.VMEM((rows, two_ns), BF16),
                        pltpu.VMEM((bsz, two_ns), F32),
                        pltpu.VMEM((rows + bsz, LANES), F32),
                        pltpu.VMEM((rows, LANES), F32),
                        pltpu.VMEM((rows, LANES), F32)],
        compiler_params=_params("parallel", "arbitrary"),
        name="ssm_core",
    )(proj3, bmat, cmat, cbmat, a2_re, a2_im, d_row)


def _glu_out_kernel(y_ref, z_ref, wg_ref, bg_ref, wo_ref, x_ref, o_ref):
    y = jax.nn.gelu(y_ref[...])
    y = y * jax.nn.sigmoid(_mm(y.astype(BF16), wg_ref[...]) + bg_ref[...])
    y = y * jax.nn.silu(z_ref[...])
    o_ref[...] = x_ref[...] + _mm(y.astype(BF16), wo_ref[...])


def _glu_out(y2d, proj, w_glu, b_glu, w_out, x2d):
    t, di = y2d.shape
    d = x2d.shape[1]
    return pl.pallas_call(
        _glu_out_kernel,
        grid=(t // ROW_TILE,),
        in_specs=[pl.BlockSpec((ROW_TILE, di), lambda i: (i, 0)),
                  pl.BlockSpec((ROW_TILE, di), lambda i: (i, 1)),
                  pl.BlockSpec((di, di), lambda i: (0, 0)),
                  pl.BlockSpec((1, di), lambda i: (0, 0)),
                  pl.BlockSpec((di, d), lambda i: (0, 0)),
                  pl.BlockSpec((ROW_TILE, d), lambda i: (i, 0))],
        out_specs=pl.BlockSpec((ROW_TILE, d), lambda i: (i, 0)),
        out_shape=jax.ShapeDtypeStruct((t, d), F32),
        compiler_params=_params("parallel"),
        name="glu_out",
    )(y2d, proj, w_glu, b_glu.reshape(1, di), w_out, x2d)


def _s5_discretize(lam_re, lam_im, log_dt, b_re, b_im, c_re, c_im):
    g, p = lam_re.shape
    i = b_re.shape[-1]
    gl = LANES // i
    nblk = g // gl
    ns = gl * p
    dt = jnp.exp(log_dt)[:, None]
    mag = jnp.exp(lam_re * dt)
    a_re = mag * jnp.cos(lam_im * dt)
    a_im = mag * jnp.sin(lam_im * dt)
    den = lam_re * lam_re + lam_im * lam_im
    bc_re = ((a_re - 1.0) * lam_re + a_im * lam_im) / den
    bc_im = (a_im * lam_re - (a_re - 1.0) * lam_im) / den
    bb_re = bc_re[..., None] * b_re - bc_im[..., None] * b_im
    bb_im = bc_re[..., None] * b_im + bc_im[..., None] * b_re
    ab_re = a_re[..., None] * bb_re - a_im[..., None] * bb_im
    ab_im = a_re[..., None] * bb_im + a_im[..., None] * bb_re
    ca_re = c_re * a_re[:, None, :] - c_im * a_im[:, None, :]
    ca_im = c_re * a_im[:, None, :] + c_im * a_re[:, None, :]
    cb = (jnp.einsum('gop,gpi->goi', c_re, bb_re, precision=lax.Precision.HIGHEST)
          - jnp.einsum('gop,gpi->goi', c_im, bb_im, precision=lax.Precision.HIGHEST))
    eye = jnp.eye(gl, dtype=F32)

    def block_diag(m):
        _, _, r, c = m.shape
        return (m[:, :, :, None, :] * eye[None, :, None, :, None]).reshape(nblk, gl * r, gl * c)

    def pack_b(m):
        return block_diag(jnp.transpose(m.reshape(nblk, gl, p, i), (0, 1, 3, 2)))

    def pack_c(m):
        return block_diag(jnp.transpose(m.reshape(nblk, gl, i, p), (0, 1, 3, 2)))

    def pack_d(m):
        return block_diag(jnp.transpose(m.reshape(nblk, gl, i, i), (0, 1, 3, 2)))

    bmat = jnp.concatenate(
        [jnp.concatenate([pack_b(ab_re), pack_b(ab_im)], axis=2),
         jnp.concatenate([pack_b(bb_re), pack_b(bb_im)], axis=2)], axis=1)
    cmat = jnp.concatenate(
        [jnp.concatenate([pack_c(ca_re), pack_c(-ca_im)], axis=1),
         jnp.concatenate([pack_c(c_re), pack_c(-c_im)], axis=1)], axis=2)
    a2_re = a_re * a_re - a_im * a_im
    a2_im = 2.0 * a_re * a_im
    return (bmat.astype(BF16), cmat.astype(BF16), pack_d(cb).astype(BF16),
            a2_re.reshape(nblk, 1, ns), a2_im.reshape(nblk, 1, ns))


def _s5_layer(x2d, bsz, seqlen, norm_g, w_in, lam_re, lam_im, log_dt, b_re, b_im,
              c_re, c_im, d_skip, w_glu, b_glu, w_out):
    d_inner = w_out.shape[0]
    proj = _norm_proj(x2d, norm_g, w_in.astype(BF16))
    bmat, cmat, cbmat, a2_re, a2_im = _s5_discretize(lam_re, lam_im, log_dt, b_re, b_im, c_re, c_im)
    y = _ssm_core(proj.reshape(bsz, seqlen, 2 * d_inner), bmat, cmat, cbmat, a2_re, a2_im,
                  d_skip.reshape(1, d_inner))
    return _glu_out(y.reshape(bsz * seqlen, d_inner), proj, w_glu.astype(BF16), b_glu,
                    w_out.astype(BF16), x2d)


def kernel(x, gla_norm, gla_w_in, gla_w_gate_up, gla_b_gate, gla_head_gain, gla_w_out,
           s5_norm, s5_w_in, s5_lam_re, s5_lam_im, s5_log_dt, s5_b_re, s5_b_im,
           s5_c_re, s5_c_im, s5_d, s5_w_glu, s5_b_glu, s5_w_out, final_norm):
    bsz, seqlen, d = x.shape
    n_gla, n_s5 = gla_w_in.shape[0], s5_w_in.shape[0]
    x2d = x.reshape(bsz * seqlen, d)
    for i in range(n_gla + n_s5):
        j = i // 2
        if i % 2 == 0:
            x2d = _gla_layer(x2d, bsz, seqlen, gla_norm[j], gla_w_in[j], gla_w_gate_up[j],
                             gla_b_gate[j], gla_head_gain[j], gla_w_out[j])
        else:
            x2d = _s5_layer(x2d, bsz, seqlen, s5_norm[j], s5_w_in[j], s5_lam_re[j], s5_lam_im[j],
                            s5_log_dt[j], s5_b_re[j], s5_b_im[j], s5_c_re[j], s5_c_im[j], s5_d[j],
                            s5_w_glu[j], s5_b_glu[j], s5_w_out[j])
    return _final_norm(x2d, final_norm).reshape(bsz, seqlen, d)
```

```python
import functools
import math

import jax
import jax.numpy as jnp
from jax import lax
from jax.experimental import pallas as pl
from jax.experimental.pallas import tpu as pltpu

F32 = jnp.float32
BF16 = jnp.bfloat16

EPS = 1e-6
GLA_HEADS = 4
GLA_GATE_TAU = 16.0
GLA_CHUNK = 64
S5_GROUP = 16
S5_STATE = 64

LANES = 128
SUBLANES = 8
VMEM_LIMIT_BYTES = 56 * 1024 * 1024

ROW_TILE = 256
GLA_TILE = 256
SSM_TILE = 256
SSM_SUB_PAIRS = 32


def _params(*sem):
    return pltpu.CompilerParams(dimension_semantics=sem, vmem_limit_bytes=VMEM_LIMIT_BYTES)


def _mm(a, b):
    return jnp.dot(a, b, preferred_element_type=F32)


def _mm_nt(a, b):
    return lax.dot_general(a, b, (((1,), (1,)), ((), ())), preferred_element_type=F32)


def _mm_tn(a, b):
    return lax.dot_general(a, b, (((0,), (0,)), ((), ())), preferred_element_type=F32)


def _rms(x, g):
    return x * lax.rsqrt(jnp.mean(x * x, axis=-1, keepdims=True) + EPS) * g


def _norm_proj_kernel(x_ref, g_ref, w_ref, o_ref):
    h = _rms(x_ref[...], g_ref[...])
    o_ref[...] = _mm(h.astype(BF16), w_ref[...])


def _norm_proj(x2d, g, w_bf16):
    t, d = x2d.shape
    n = w_bf16.shape[1]
    return pl.pallas_call(
        _norm_proj_kernel,
        grid=(t // ROW_TILE,),
        in_specs=[pl.BlockSpec((ROW_TILE, d), lambda i: (i, 0)),
                  pl.BlockSpec((1, d), lambda i: (0, 0)),
                  pl.BlockSpec((d, n), lambda i: (0, 0))],
        out_specs=pl.BlockSpec((ROW_TILE, n), lambda i: (i, 0)),
        out_shape=jax.ShapeDtypeStruct((t, n), F32),
        compiler_params=_params("parallel"),
        name="norm_proj",
    )(x2d, g.reshape(1, d), w_bf16)


def _final_norm_kernel(x_ref, g_ref, o_ref):
    o_ref[...] = _rms(x_ref[...], g_ref[...])


def _final_norm(x2d, g):
    t, d = x2d.shape
    return pl.pallas_call(
        _final_norm_kernel,
        grid=(t // ROW_TILE,),
        in_specs=[pl.BlockSpec((ROW_TILE, d), lambda i: (i, 0)),
                  pl.BlockSpec((1, d), lambda i: (0, 0))],
        out_specs=pl.BlockSpec((ROW_TILE, d), lambda i: (i, 0)),
        out_shape=jax.ShapeDtypeStruct((t, d), F32),
        compiler_params=_params("parallel"),
        name="final_norm",
    )(x2d, g.reshape(1, d))


def _split3_bf16(x):
    hi = x.astype(BF16)
    r1 = x - hi.astype(F32)
    mid = r1.astype(BF16)
    lo = (r1 - mid.astype(F32)).astype(BF16)
    return hi, mid, lo


def _gla_layer_kernel(x_ref, g_ref, win_ref, wup_ref, bg_ref, gain_ref, wout_ref, o_ref,
                      proj_ref, bcum_ref, og_ref, s_ref, *, heads, dk, dv):
    tl = x_ref.shape[0]
    c = GLA_CHUNK
    kd = heads * dk
    di = heads * dv
    q0, k0, v0, z0, r0 = 0, kd, 2 * kd, 2 * kd + di, 2 * kd + 2 * di

    @pl.when(pl.program_id(1) == 0)
    def _():
        s_ref[...] = jnp.zeros_like(s_ref)

    h = _rms(x_ref[...], g_ref[...])
    proj_ref[...] = _mm(h.astype(BF16), win_ref[...])

    gate = _mm(proj_ref[:, r0:r0 + LANES].astype(BF16), wup_ref[...]) + bg_ref[...]
    log_a = jax.nn.log_sigmoid(gate) / GLA_GATE_TAU

    row = lax.broadcasted_iota(jnp.int32, (tl, tl), 0)
    col = lax.broadcasted_iota(jnp.int32, (tl, tl), 1)
    ltri = jnp.where((row // c == col // c) & (col <= row), 1.0, 0.0).astype(BF16)
    hi, mid, lo = _split3_bf16(log_a)
    bcum_ref[...] = _mm(ltri, hi) + _mm(ltri, mid) + _mm(ltri, lo)

    crow = lax.broadcasted_iota(jnp.int32, (c, c), 0)
    ccol = lax.broadcasted_iota(jnp.int32, (c, c), 1)
    causal = ccol <= crow
    drow = lax.broadcasted_iota(jnp.int32, (dk, dk), 0)
    dcol = lax.broadcasted_iota(jnp.int32, (dk, dk), 1)
    eye = drow == dcol
    scale = dk ** -0.5
    gain = gain_ref[...]

    for ci in range(tl // c):
        sl = slice(ci * c, (ci + 1) * c)
        for hh in range(heads):
            bc = bcum_ref[sl, hh * dk:(hh + 1) * dk]
            bl = bc[c - 1:c, :]
            q = proj_ref[sl, q0 + hh * dk:q0 + (hh + 1) * dk]
            k = proj_ref[sl, k0 + hh * dk:k0 + (hh + 1) * dk]
            qg = (q * scale) * jnp.exp(bc)
            kg = k * jnp.exp(-bc)
            ke = k * jnp.exp(bl - bc)
            vb = proj_ref[sl, v0 + hh * dv:v0 + (hh + 1) * dv].astype(BF16)
            qgb = qg.astype(BF16)

            att = jnp.where(causal, _mm_nt(qgb, kg.astype(BF16)), 0.0)
            s_prev = s_ref[hh]
            o = _mm(att.astype(BF16), vb) + _mm(qgb, s_prev.astype(BF16))

            dec = jnp.exp(bl)
            dec_col = jnp.sum(jnp.where(eye, jnp.broadcast_to(dec, (dk, dk)), 0.0),
                              axis=1, keepdims=True)
            s_ref[hh] = dec_col * s_prev + _mm_tn(ke.astype(BF16), vb)

            z = proj_ref[sl, z0 + hh * dv:z0 + (hh + 1) * dv]
            og_ref[sl, hh * dv:(hh + 1) * dv] = (_rms(o, gain) * jax.nn.silu(z)).astype(BF16)

    o_ref[...] = x_ref[...] + _mm(og_ref[...], wout_ref[...])


def _resident(shape):
    return pl.BlockSpec(shape, lambda *_: (0,) * len(shape), pipeline_mode=pl.Buffered(1))


def _gla_layer(x2d, bsz, seqlen, norm_g, w_in, w_gate_up, b_gate, head_gain, w_out):
    t, d = x2d.shape
    rank, kd = w_gate_up.shape
    d_inner = w_out.shape[0]
    dk = kd // GLA_HEADS
    dv = d_inner // GLA_HEADS
    n_proj = 2 * kd + 2 * d_inner + LANES
    nt = seqlen // GLA_TILE
    w_in_pad = jnp.concatenate(
        [w_in, jnp.zeros((d, LANES - rank), w_in.dtype)], axis=1).astype(BF16)
    w_up_pad = jnp.concatenate(
        [w_gate_up, jnp.zeros((LANES - rank, kd), w_gate_up.dtype)], axis=0).astype(BF16)
    return pl.pallas_call(
        functools.partial(_gla_layer_kernel, heads=GLA_HEADS, dk=dk, dv=dv),
        grid=(bsz, nt),
        in_specs=[pl.BlockSpec((GLA_TILE, d), lambda b, i: (b * nt + i, 0)),
                  _resident((1, d)),
                  _resident((d, n_proj)),
                  _resident((LANES, kd)),
                  _resident((1, kd)),
                  _resident((1, dv)),
                  _resident((d_inner, d))],
        out_specs=pl.BlockSpec((GLA_TILE, d), lambda b, i: (b * nt + i, 0)),
        out_shape=jax.ShapeDtypeStruct((t, d), F32),
        scratch_shapes=[pltpu.VMEM((GLA_TILE, n_proj), F32),
                        pltpu.VMEM((GLA_TILE, kd), F32),
                        pltpu.VMEM((GLA_TILE, d_inner), BF16),
                        pltpu.VMEM((GLA_HEADS, dk, dv), F32)],
        compiler_params=_params("parallel", "arbitrary"),
        name="gla_layer",
    )(x2d, norm_g.reshape(1, d), w_in_pad, w_up_pad, b_gate.reshape(1, kd),
      head_gain.reshape(1, dv), w_out.astype(BF16))


def _ssm_kernel(u_ref, bm_ref, cm_ref, cb_ref, a2re_ref, a2im_ref, d_ref, y_ref,
                ue_ref, uo_ref, u2_ref, w_ref, tb_ref, st_ref, zsh_ref, ye_ref, yo_ref):
    nb, tl, lanes = u_ref.shape
    npair = tl // 2
    rows = npair * nb
    ns = st_ref.shape[1] // 2
    sub = SSM_SUB_PAIRS
    sub_rows = sub * nb

    @pl.when(pl.program_id(1) == 0)
    def _():
        st_ref[...] = jnp.zeros_like(st_ref)
        zsh_ref[0:nb, :] = jnp.zeros((nb, lanes), F32)

    for b in range(nb):
        ue_ref[pl.ds(b, npair, stride=nb), :] = u_ref[b, pl.ds(0, npair, stride=2), :]
        uo_ref[pl.ds(b, npair, stride=nb), :] = u_ref[b, pl.ds(1, npair, stride=2), :]

    u2_ref[...] = jnp.concatenate([ue_ref[...], uo_ref[...]], axis=1).astype(BF16)

    a_re = jnp.broadcast_to(a2re_ref[0], (nb, ns))
    a_im = jnp.broadcast_to(a2im_ref[0], (nb, ns))
    d = d_ref[...]
    s_re = st_ref[:, 0:ns]
    s_im = st_ref[:, ns:2 * ns]

    for k in range(npair // sub):
        blk = slice(k * sub_rows, (k + 1) * sub_rows)
        w_ref[blk, :] = _mm(u2_ref[blk, :], bm_ref[0])
        for m in range(0, sub, 2):
            r0 = k * sub_rows + m * nb
            m_re = a_re * s_re - a_im * s_im + w_ref[r0:r0 + nb, 0:ns]
            m_im = a_re * s_im + a_im * s_re + w_ref[r0:r0 + nb, ns:2 * ns]
            s_re = a_re * m_re - a_im * m_im + w_ref[r0 + nb:r0 + 2 * nb, 0:ns]
            s_im = a_re * m_im + a_im * m_re + w_ref[r0 + nb:r0 + 2 * nb, ns:2 * ns]
            tb_ref[r0:r0 + 2 * nb, 0:ns] = jnp.concatenate([m_re, s_re], axis=0).astype(BF16)
            tb_ref[r0:r0 + 2 * nb, ns:2 * ns] = jnp.concatenate([m_im, s_im], axis=0).astype(BF16)
        z = _mm(tb_ref[blk, :], cm_ref[0])
        zsh_ref[nb + k * sub_rows:nb + (k + 1) * sub_rows, :] = z[:, 0:lanes]
        ye_ref[blk, :] = (zsh_ref[blk, :] + _mm(u2_ref[blk, 0:lanes], cb_ref[0])
                          + d * ue_ref[blk, :])
        yo_ref[blk, :] = z[:, lanes:2 * lanes] + d * uo_ref[blk, :]

    st_ref[:, 0:ns] = s_re
    st_ref[:, ns:2 * ns] = s_im
    zsh_ref[0:nb, :] = zsh_ref[rows:rows + nb, :]
    for b in range(nb):
        y_ref[b, pl.ds(0, npair, stride=2), :] = ye_ref[pl.ds(b, npair, stride=nb), :]
        y_ref[b, pl.ds(1, npair, stride=2), :] = yo_ref[pl.ds(b, npair, stride=nb), :]


def _ssm_core(proj3, bmat, cmat, cbmat, a2_re, a2_im, d_row):
    bsz, seqlen, _ = proj3.shape
    nblk, _, two_ns = bmat.shape
    d_inner = nblk * LANES
    tl = SSM_TILE
    rows = (tl // 2) * bsz
    return pl.pallas_call(
        _ssm_kernel,
        grid=(nblk, seqlen // tl),
        in_specs=[
            pl.BlockSpec((bsz, tl, LANES), lambda j, i: (0, i, j)),
            pl.BlockSpec((1, 2 * LANES, two_ns), lambda j, i: (j, 0, 0)),
            pl.BlockSpec((1, two_ns, 2 * LANES), lambda j, i: (j, 0, 0)),
            pl.BlockSpec((1, LANES, LANES), lambda j, i: (j, 0, 0)),
            pl.BlockSpec((1, 1, two_ns // 2), lambda j, i: (j, 0, 0)),
            pl.BlockSpec((1, 1, two_ns // 2), lambda j, i: (j, 0, 0)),
            pl.BlockSpec((1, LANES), lambda j, i: (0, j)),
        ],
        out_specs=pl.BlockSpec((bsz, tl, LANES), lambda j, i: (0, i, j)),
        out_shape=jax.ShapeDtypeStruct((bsz, seqlen, d_inner), F32),
        scratch_shapes=[pltpu.VMEM((rows, LANES), F32),
                        pltpu.VMEM((rows, LANES), F32),
                        pltpu.VMEM((rows, 2 * LANES), BF16),
                        pltpu.VMEM((rows, two_ns), F32),
                        pltpu.VMEM((rows, two_ns), BF16),
                        pltpu.VMEM((bsz, two_ns), F32),
                        pltpu.VMEM((rows + bsz, LANES), F32),
                        pltpu.VMEM((rows, LANES), F32),
                        pltpu.VMEM((rows, LANES), F32)],
        compiler_params=_params("parallel", "arbitrary"),
        name="ssm_core",
    )(proj3, bmat, cmat, cbmat, a2_re, a2_im, d_row)


def _glu_out_kernel(y_ref, z_ref, wg_ref, bg_ref, wo_ref, x_ref, o_ref):
    y = jax.nn.gelu(y_ref[...])
    y = y * jax.nn.sigmoid(_mm(y.astype(BF16), wg_ref[...]) + bg_ref[...])
    y = y * jax.nn.silu(z_ref[...])
    o_ref[...] = x_ref[...] + _mm(y.astype(BF16), wo_ref[...])


def _glu_out(y2d, proj, w_glu, b_glu, w_out, x2d):
    t, di = y2d.shape
    d = x2d.shape[1]
    return pl.pallas_call(
        _glu_out_kernel,
        grid=(t // ROW_TILE,),
        in_specs=[pl.BlockSpec((ROW_TILE, di), lambda i: (i, 0)),
                  pl.BlockSpec((ROW_TILE, di), lambda i: (i, 1)),
                  pl.BlockSpec((di, di), lambda i: (0, 0)),
                  pl.BlockSpec((1, di), lambda i: (0, 0)),
                  pl.BlockSpec((di, d), lambda i: (0, 0)),
                  pl.BlockSpec((ROW_TILE, d), lambda i: (i, 0))],
        out_specs=pl.BlockSpec((ROW_TILE, d), lambda i: (i, 0)),
        out_shape=jax.ShapeDtypeStruct((t, d), F32),
        compiler_params=_params("parallel"),
        name="glu_out",
    )(y2d, proj, w_glu, b_glu.reshape(1, di), w_out, x2d)


def _s5_discretize(lam_re, lam_im, log_dt, b_re, b_im, c_re, c_im):
    g, p = lam_re.shape
    i = b_re.shape[-1]
    gl = LANES // i
    nblk = g // gl
    ns = gl * p
    dt = jnp.exp(log_dt)[:, None]
    mag = jnp.exp(lam_re * dt)
    a_re = mag * jnp.cos(lam_im * dt)
    a_im = mag * jnp.sin(lam_im * dt)
    den = lam_re * lam_re + lam_im * lam_im
    bc_re = ((a_re - 1.0) * lam_re + a_im * lam_im) / den
    bc_im = (a_im * lam_re - (a_re - 1.0) * lam_im) / den
    bb_re = bc_re[..., None] * b_re - bc_im[..., None] * b_im
    bb_im = bc_re[..., None] * b_im + bc_im[..., None] * b_re
    ab_re = a_re[..., None] * bb_re - a_im[..., None] * bb_im
    ab_im = a_re[..., None] * bb_im + a_im[..., None] * bb_re
    ca_re = c_re * a_re[:, None, :] - c_im * a_im[:, None, :]
    ca_im = c_re * a_im[:, None, :] + c_im * a_re[:, None, :]
    cb = (jnp.einsum('gop,gpi->goi', c_re, bb_re, precision=lax.Precision.HIGHEST)
          - jnp.einsum('gop,gpi->goi', c_im, bb_im, precision=lax.Precision.HIGHEST))
    eye = jnp.eye(gl, dtype=F32)

    def block_diag(m):
        _, _, r, c = m.shape
        return (m[:, :, :, None, :] * eye[None, :, None, :, None]).reshape(nblk, gl * r, gl * c)

    def pack_b(m):
        return block_diag(jnp.transpose(m.reshape(nblk, gl, p, i), (0, 1, 3, 2)))

    def pack_c(m):
        return block_diag(jnp.transpose(m.reshape(nblk, gl, i, p), (0, 1, 3, 2)))

    def pack_d(m):
        return block_diag(jnp.transpose(m.reshape(nblk, gl, i, i), (0, 1, 3, 2)))

    bmat = jnp.concatenate(
        [jnp.concatenate([pack_b(ab_re), pack_b(ab_im)], axis=2),
         jnp.concatenate([pack_b(bb_re), pack_b(bb_im)], axis=2)], axis=1)
    cmat = jnp.concatenate(
        [jnp.concatenate([pack_c(ca_re), pack_c(-ca_im)], axis=1),
         jnp.concatenate([pack_c(c_re), pack_c(-c_im)], axis=1)], axis=2)
    a2_re = a_re * a_re - a_im * a_im
    a2_im = 2.0 * a_re * a_im
    return (bmat.astype(BF16), cmat.astype(BF16), pack_d(cb).astype(BF16),
            a2_re.reshape(nblk, 1, ns), a2_im.reshape(nblk, 1, ns))


def _s5_layer(x2d, bsz, seqlen, norm_g, w_in, lam_re, lam_im, log_dt, b_re, b_im,
              c_re, c_im, d_skip, w_glu, b_glu, w_out):
    d_inner = w_out.shape[0]
    proj = _norm_proj(x2d, norm_g, w_in.astype(BF16))
    bmat, cmat, cbmat, a2_re, a2_im = _s5_discretize(lam_re, lam_im, log_dt, b_re, b_im, c_re, c_im)
    y = _ssm_core(proj.reshape(bsz, seqlen, 2 * d_inner), bmat, cmat, cbmat, a2_re, a2_im,
                  d_skip.reshape(1, d_inner))
    return _glu_out(y.reshape(bsz * seqlen, d_inner), proj, w_glu.astype(BF16), b_glu,
                    w_out.astype(BF16), x2d)


def kernel(x, gla_norm, gla_w_in, gla_w_gate_up, gla_b_gate, gla_head_gain, gla_w_out,
           s5_norm, s5_w_in, s5_lam_re, s5_lam_im, s5_log_dt, s5_b_re, s5_b_im,
           s5_c_re, s5_c_im, s5_d, s5_w_glu, s5_b_glu, s5_w_out, final_norm):
    bsz, seqlen, d = x.shape
    n_gla, n_s5 = gla_w_in.shape[0], s5_w_in.shape[0]
    x2d = x.reshape(bsz * seqlen, d)
    for i in range(n_gla + n_s5):
        j = i // 2
        if i % 2 == 0:
            x2d = _gla_layer(x2d, bsz, seqlen, gla_norm[j], gla_w_in[j], gla_w_gate_up[j],
                             gla_b_gate[j], gla_head_gain[j], gla_w_out[j])
        else:
            x2d = _s5_layer(x2d, bsz, seqlen, s5_norm[j], s5_w_in[j], s5_lam_re[j], s5_lam_im[j],
                            s5_log_dt[j], s5_b_re[j], s5_b_im[j], s5_c_re[j], s5_c_im[j], s5_d[j],
                            s5_w_glu[j], s5_b_glu[j], s5_w_out[j])
    return _final_norm(x2d, final_norm).reshape(bsz, seqlen, d)
```

```python
import functools
import math

import jax
import jax.numpy as jnp
from jax import lax
from jax.experimental import pallas as pl
from jax.experimental.pallas import tpu as pltpu

F32 = jnp.float32
BF16 = jnp.bfloat16

EPS = 1e-6
GLA_HEADS = 4
GLA_GATE_TAU = 16.0
GLA_CHUNK = 64
S5_GROUP = 16
S5_STATE = 64

LANES = 128
SUBLANES = 8
VMEM_LIMIT_BYTES = 56 * 1024 * 1024

ROW_TILE = 256
GLA_TILE = 256
SSM_TILE = 256
SSM_SUB_PAIRS = 32


def _params(*sem):
    return pltpu.CompilerParams(dimension_semantics=sem, vmem_limit_bytes=VMEM_LIMIT_BYTES)


def _mm(a, b):
    return jnp.dot(a, b, preferred_element_type=F32)


def _mm_nt(a, b):
    return lax.dot_general(a, b, (((1,), (1,)), ((), ())), preferred_element_type=F32)


def _mm_tn(a, b):
    return lax.dot_general(a, b, (((0,), (0,)), ((), ())), preferred_element_type=F32)


def _rms(x, g):
    return x * lax.rsqrt(jnp.mean(x * x, axis=-1, keepdims=True) + EPS) * g


def _norm_proj_kernel(x_ref, g_ref, w_ref, o_ref):
    h = _rms(x_ref[...], g_ref[...])
    o_ref[...] = _mm(h.astype(BF16), w_ref[...])


def _norm_proj(x2d, g, w_bf16):
    t, d = x2d.shape
    n = w_bf16.shape[1]
    return pl.pallas_call(
        _norm_proj_kernel,
        grid=(t // ROW_TILE,),
        in_specs=[pl.BlockSpec((ROW_TILE, d), lambda i: (i, 0)),
                  pl.BlockSpec((1, d), lambda i: (0, 0)),
                  pl.BlockSpec((d, n), lambda i: (0, 0))],
        out_specs=pl.BlockSpec((ROW_TILE, n), lambda i: (i, 0)),
        out_shape=jax.ShapeDtypeStruct((t, n), F32),
        compiler_params=_params("parallel"),
        name="norm_proj",
    )(x2d, g.reshape(1, d), w_bf16)


def _final_norm_kernel(x_ref, g_ref, o_ref):
    o_ref[...] = _rms(x_ref[...], g_ref[...])


def _final_norm(x2d, g):
    t, d = x2d.shape
    return pl.pallas_call(
        _final_norm_kernel,
        grid=(t // ROW_TILE,),
        in_specs=[pl.BlockSpec((ROW_TILE, d), lambda i: (i, 0)),
                  pl.BlockSpec((1, d), lambda i: (0, 0))],
        out_specs=pl.BlockSpec((ROW_TILE, d), lambda i: (i, 0)),
        out_shape=jax.ShapeDtypeStruct((t, d), F32),
        compiler_params=_params("parallel"),
        name="final_norm",
    )(x2d, g.reshape(1, d))


def _split3_bf16(x):
    hi = x.astype(BF16)
    r1 = x - hi.astype(F32)
    mid = r1.astype(BF16)
    lo = (r1 - mid.astype(F32)).astype(BF16)
    return hi, mid, lo


def _gla_layer_kernel(x_ref, g_ref, win_ref, wr_ref, wup_ref, bg_ref, gain_ref, wout_ref, o_ref,
                      proj_ref, bcum_ref, og_ref, s_ref, *, heads, dk, dv):
    tl = x_ref.shape[0]
    c = GLA_CHUNK
    kd = heads * dk
    di = heads * dv
    q0, k0, v0, z0 = 0, kd, 2 * kd, 2 * kd + di

    @pl.when(pl.program_id(1) == 0)
    def _():
        s_ref[...] = jnp.zeros_like(s_ref)

    h = _rms(x_ref[...], g_ref[...]).astype(BF16)
    proj_ref[...] = _mm(h, win_ref[...])

    gate = _mm(_mm(h, wr_ref[...]).astype(BF16), wup_ref[...]) + bg_ref[...]
    log_a = jax.nn.log_sigmoid(gate) / GLA_GATE_TAU

    row = lax.broadcasted_iota(jnp.int32, (tl, tl), 0)
    col = lax.broadcasted_iota(jnp.int32, (tl, tl), 1)
    ltri = jnp.where((row // c == col // c) & (col <= row), 1.0, 0.0).astype(BF16)
    hi, mid, lo = _split3_bf16(log_a)
    bcum_ref[...] = _mm(ltri, hi) + _mm(ltri, mid) + _mm(ltri, lo)

    crow = lax.broadcasted_iota(jnp.int32, (c, c), 0)
    ccol = lax.broadcasted_iota(jnp.int32, (c, c), 1)
    causal = ccol <= crow
    drow = lax.broadcasted_iota(jnp.int32, (dk, dk), 0)
    dcol = lax.broadcasted_iota(jnp.int32, (dk, dk), 1)
    eye = drow == dcol
    scale = dk ** -0.5
    gain = gain_ref[...]

    for ci in range(tl // c):
        sl = slice(ci * c, (ci + 1) * c)
        for hh in range(heads):
            bc = bcum_ref[sl, hh * dk:(hh + 1) * dk]
            bl = bc[c - 1:c, :]
            q = proj_ref[sl, q0 + hh * dk:q0 + (hh + 1) * dk]
            k = proj_ref[sl, k0 + hh * dk:k0 + (hh + 1) * dk]
            qg = (q * scale) * jnp.exp(bc)
            kg = k * jnp.exp(-bc)
            ke = k * jnp.exp(bl - bc)
            vb = proj_ref[sl, v0 + hh * dv:v0 + (hh + 1) * dv].astype(BF16)
            qgb = qg.astype(BF16)

            att = jnp.where(causal, _mm_nt(qgb, kg.astype(BF16)), 0.0)
            s_prev = s_ref[hh]
            o = _mm(att.astype(BF16), vb) + _mm(qgb, s_prev.astype(BF16))

            dec = jnp.exp(bl)
            dec_col = jnp.sum(jnp.where(eye, jnp.broadcast_to(dec, (dk, dk)), 0.0),
                              axis=1, keepdims=True)
            s_ref[hh] = dec_col * s_prev + _mm_tn(ke.astype(BF16), vb)

            z = proj_ref[sl, z0 + hh * dv:z0 + (hh + 1) * dv]
            og_ref[sl, hh * dv:(hh + 1) * dv] = (_rms(o, gain) * jax.nn.silu(z)).astype(BF16)

    o_ref[...] = x_ref[...] + _mm(og_ref[...], wout_ref[...])


def _resident(shape):
    return pl.BlockSpec(shape, lambda *_: (0,) * len(shape), pipeline_mode=pl.Buffered(1))


def _gla_layer(x2d, bsz, seqlen, norm_g, w_in, w_gate_up, b_gate, head_gain, w_out):
    t, d = x2d.shape
    rank, kd = w_gate_up.shape
    d_inner = w_out.shape[0]
    dk = kd // GLA_HEADS
    dv = d_inner // GLA_HEADS
    n_proj = 2 * kd + 2 * d_inner
    nt = seqlen // GLA_TILE
    w_main = w_in[:, :n_proj].astype(BF16)
    w_r_pad = jnp.concatenate(
        [w_in[:, n_proj:], jnp.zeros((d, LANES - rank), w_in.dtype)], axis=1).astype(BF16)
    w_up_pad = jnp.concatenate(
        [w_gate_up, jnp.zeros((LANES - rank, kd), w_gate_up.dtype)], axis=0).astype(BF16)
    return pl.pallas_call(
        functools.partial(_gla_layer_kernel, heads=GLA_HEADS, dk=dk, dv=dv),
        grid=(bsz, nt),
        in_specs=[pl.BlockSpec((GLA_TILE, d), lambda b, i: (b * nt + i, 0)),
                  _resident((1, d)),
                  _resident((d, n_proj)),
                  _resident((d, LANES)),
                  _resident((LANES, kd)),
                  _resident((1, kd)),
                  _resident((1, dv)),
                  _resident((d_inner, d))],
        out_specs=pl.BlockSpec((GLA_TILE, d), lambda b, i: (b * nt + i, 0)),
        out_shape=jax.ShapeDtypeStruct((t, d), F32),
        scratch_shapes=[pltpu.VMEM((GLA_TILE, n_proj), F32),
                        pltpu.VMEM((GLA_TILE, kd), F32),
                        pltpu.VMEM((GLA_TILE, d_inner), BF16),
                        pltpu.VMEM((GLA_HEADS, dk, dv), F32)],
        compiler_params=_params("parallel", "arbitrary"),
        name="gla_layer",
    )(x2d, norm_g.reshape(1, d), w_main, w_r_pad, w_up_pad, b_gate.reshape(1, kd),
      head_gain.reshape(1, dv), w_out.astype(BF16))


def _iota2(shape):
    return (lax.broadcasted_iota(jnp.int32, shape, 0), lax.broadcasted_iota(jnp.int32, shape, 1))


def _ssm_expand(bc_ref, cc_ref, dc_ref, bm_ref, cm_ref, cb_ref):
    gi = dc_ref.shape[2]
    ns = bc_ref.shape[3] // 2
    gstates = ns // (LANES // gi)
    bc = bc_ref[0, 0].astype(BF16)
    cc = cc_ref[0, 0].astype(BF16)
    dc = dc_ref[0, 0].astype(BF16)

    r, c = _iota2((2 * LANES, 2 * gi))
    sel = jnp.where((r // LANES == c // gi) & (r % gi == c % gi), 1.0, 0.0).astype(BF16)
    r, c = _iota2((2 * LANES, 2 * ns))
    keep = (r % LANES) // gi == (c % ns) // gstates
    bm_ref[...] = jnp.where(keep, _mm(sel, bc), 0.0).astype(BF16)

    r, c = _iota2((2 * gi, 2 * LANES))
    sel_t = jnp.where((c // LANES == r // gi) & (c % gi == r % gi), 1.0, 0.0).astype(BF16)
    r, c = _iota2((2 * ns, 2 * LANES))
    keep = (r % ns) // gstates == (c % LANES) // gi
    cm_ref[...] = jnp.where(keep, _mm_tn(cc, sel_t), 0.0).astype(BF16)

    r, c = _iota2((gi, LANES))
    sel_o = jnp.where(c % gi == r, 1.0, 0.0).astype(BF16)
    r, c = _iota2((LANES, LANES))
    cb_ref[...] = jnp.where(r // gi == c // gi, _mm_tn(dc, sel_o), 0.0).astype(BF16)


def _ssm_kernel(u_ref, bc_ref, cc_ref, dc_ref, a2re_ref, a2im_ref, d_ref, y_ref,
                bm_ref, cm_ref, cb_ref, ue_ref, uo_ref, u2_ref, w_ref, tb_ref, st_ref, zsh_ref,
                ye_ref, yo_ref):
    nb, tl, lanes = u_ref.shape
    npair = tl // 2
    rows = npair * nb
    ns = st_ref.shape[1] // 2
    sub = SSM_SUB_PAIRS
    sub_rows = sub * nb

    @pl.when(pl.program_id(1) == 0)
    def _():
        st_ref[...] = jnp.zeros_like(st_ref)
        zsh_ref[0:nb, :] = jnp.zeros((nb, lanes), F32)
        _ssm_expand(bc_ref, cc_ref, dc_ref, bm_ref, cm_ref, cb_ref)

    for b in range(nb):
        ue_ref[pl.ds(b, npair, stride=nb), :] = u_ref[b, pl.ds(0, npair, stride=2), :]
        uo_ref[pl.ds(b, npair, stride=nb), :] = u_ref[b, pl.ds(1, npair, stride=2), :]

    u2_ref[...] = jnp.concatenate([ue_ref[...], uo_ref[...]], axis=1).astype(BF16)

    a_re = jnp.broadcast_to(a2re_ref[0, 0], (nb, ns))
    a_im = jnp.broadcast_to(a2im_ref[0, 0], (nb, ns))
    d = d_ref[0]
    s_re = st_ref[:, 0:ns]
    s_im = st_ref[:, ns:2 * ns]

    for k in range(npair // sub):
        blk = slice(k * sub_rows, (k + 1) * sub_rows)
        w_ref[blk, :] = _mm(u2_ref[blk, :], bm_ref[...])
        for m in range(0, sub, 2):
            r0 = k * sub_rows + m * nb
            m_re = a_re * s_re - a_im * s_im + w_ref[r0:r0 + nb, 0:ns]
            m_im = a_re * s_im + a_im * s_re + w_ref[r0:r0 + nb, ns:2 * ns]
            s_re = a_re * m_re - a_im * m_im + w_ref[r0 + nb:r0 + 2 * nb, 0:ns]
            s_im = a_re * m_im + a_im * m_re + w_ref[r0 + nb:r0 + 2 * nb, ns:2 * ns]
            tb_ref[r0:r0 + 2 * nb, 0:ns] = jnp.concatenate([m_re, s_re], axis=0).astype(BF16)
            tb_ref[r0:r0 + 2 * nb, ns:2 * ns] = jnp.concatenate([m_im, s_im], axis=0).astype(BF16)
        z = _mm(tb_ref[blk, :], cm_ref[...])
        zsh_ref[nb + k * sub_rows:nb + (k + 1) * sub_rows, :] = z[:, 0:lanes]
        ye_ref[blk, :] = (zsh_ref[blk, :] + _mm(u2_ref[blk, 0:lanes], cb_ref[...])
                          + d * ue_ref[blk, :])
        yo_ref[blk, :] = z[:, lanes:2 * lanes] + d * uo_ref[blk, :]

    st_ref[:, 0:ns] = s_re
    st_ref[:, ns:2 * ns] = s_im
    zsh_ref[0:nb, :] = zsh_ref[rows:rows + nb, :]
    for b in range(nb):
        y_ref[b, pl.ds(0, npair, stride=2), :] = ye_ref[pl.ds(b, npair, stride=nb), :]
        y_ref[b, pl.ds(1, npair, stride=2), :] = yo_ref[pl.ds(b, npair, stride=nb), :]


def _ssm_core(proj3, prep, layer):
    bcomp, ccomp, dcomp, a2_re, a2_im, d_row = prep
    bsz, seqlen, _ = proj3.shape
    _, nblk, gi2, two_ns = bcomp.shape
    d_inner = nblk * LANES
    tl = SSM_TILE
    rows = (tl // 2) * bsz
    per_block = lambda *blk: pl.BlockSpec((1, 1) + blk, lambda j, i: (layer, j, 0, 0))
    return pl.pallas_call(
        _ssm_kernel,
        grid=(nblk, seqlen // tl),
        in_specs=[
            pl.BlockSpec((bsz, tl, LANES), lambda j, i: (0, i, j)),
            per_block(gi2, two_ns),
            per_block(gi2, two_ns),
            per_block(gi2 // 2, LANES),
            per_block(1, two_ns // 2),
            per_block(1, two_ns // 2),
            pl.BlockSpec((1, 1, LANES), lambda j, i: (layer, 0, j)),
        ],
        out_specs=pl.BlockSpec((bsz, tl, LANES), lambda j, i: (0, i, j)),
        out_shape=jax.ShapeDtypeStruct((bsz, seqlen, d_inner), F32),
        scratch_shapes=[pltpu.VMEM((2 * LANES, two_ns), BF16),
                        pltpu.VMEM((two_ns, 2 * LANES), BF16),
                        pltpu.VMEM((LANES, LANES), BF16),
                        pltpu.VMEM((rows, LANES), F32),
                        pltpu.VMEM((rows, LANES), F32),
                        pltpu.VMEM((rows, 2 * LANES), BF16),
                        pltpu.VMEM((rows, two_ns), F32),
                        pltpu.VMEM((rows, two_ns), BF16),
                        pltpu.VMEM((bsz, two_ns), F32),
                        pltpu.VMEM((rows + bsz, LANES), F32),
                        pltpu.VMEM((rows, LANES), F32),
                        pltpu.VMEM((rows, LANES), F32)],
        compiler_params=_params("parallel", "arbitrary"),
        name="ssm_core",
    )(proj3, bcomp, ccomp, dcomp, a2_re, a2_im, d_row)


def _glu_out_kernel(y_ref, z_ref, wg_ref, bg_ref, wo_ref, x_ref, *rest):
    o_ref = rest[-1]
    y = jax.nn.gelu(y_ref[...])
    y = y * jax.nn.sigmoid(_mm(y.astype(BF16), wg_ref[...]) + bg_ref[...])
    y = y * jax.nn.silu(z_ref[...])
    out = x_ref[...] + _mm(y.astype(BF16), wo_ref[...])
    if len(rest) == 2:
        out = _rms(out, rest[0][...])
    o_ref[...] = out


def _glu_out(y2d, proj, w_glu, b_glu, w_out, x2d, final_gain=None):
    t, di = y2d.shape
    d = x2d.shape[1]
    in_specs = [pl.BlockSpec((ROW_TILE, di), lambda i: (i, 0)),
                pl.BlockSpec((ROW_TILE, di), lambda i: (i, 1)),
                _resident((di, di)),
                _resident((1, di)),
                _resident((di, d)),
                pl.BlockSpec((ROW_TILE, d), lambda i: (i, 0))]
    args = [y2d, proj, w_glu, b_glu.reshape(1, di), w_out, x2d]
    if final_gain is not None:
        in_specs.append(_resident((1, d)))
        args.append(final_gain.reshape(1, d))
    return pl.pallas_call(
        _glu_out_kernel,
        grid=(t // ROW_TILE,),
        in_specs=in_specs,
        out_specs=pl.BlockSpec((ROW_TILE, d), lambda i: (i, 0)),
        out_shape=jax.ShapeDtypeStruct((t, d), F32),
        compiler_params=_params("parallel"),
        name="glu_out",
    )(*args)


def _s5_prep(lam_re, lam_im, log_dt, b_re, b_im, c_re, c_im, d_skip):
    n, g, p = lam_re.shape
    i = b_re.shape[-1]
    gl = LANES // i
    nblk = g // gl
    ns = gl * p
    dt = jnp.exp(log_dt)[..., None]
    mag = jnp.exp(lam_re * dt)
    a_re = mag * jnp.cos(lam_im * dt)
    a_im = mag * jnp.sin(lam_im * dt)
    den = lam_re * lam_re + lam_im * lam_im
    bc_re = ((a_re - 1.0) * lam_re + a_im * lam_im) / den
    bc_im = (a_im * lam_re - (a_re - 1.0) * lam_im) / den
    bb_re = bc_re[..., None] * b_re - bc_im[..., None] * b_im
    bb_im = bc_re[..., None] * b_im + bc_im[..., None] * b_re
    ab_re = a_re[..., None] * bb_re - a_im[..., None] * bb_im
    ab_im = a_re[..., None] * bb_im + a_im[..., None] * bb_re
    ca_re = c_re * a_re[:, :, None, :] - c_im * a_im[:, :, None, :]
    ca_im = c_re * a_im[:, :, None, :] + c_im * a_re[:, :, None, :]
    cb = (jnp.einsum('ngop,ngpi->ngoi', c_re, bb_re, precision=lax.Precision.HIGHEST)
          - jnp.einsum('ngop,ngpi->ngoi', c_im, bb_im, precision=lax.Precision.HIGHEST))

    bsrc = jnp.stack([ab_re, ab_im, bb_re, bb_im], axis=1).reshape(n, 2, 2, nblk, gl, p, i)
    bcomp = jnp.transpose(bsrc, (0, 3, 1, 6, 2, 4, 5)).reshape(n, nblk, 2 * i, 2 * ns)
    csrc = jnp.stack([ca_re, -ca_im, c_re, -c_im], axis=1).reshape(n, 2, 2, nblk, gl, i, p)
    ccomp = jnp.transpose(csrc, (0, 3, 1, 5, 2, 4, 6)).reshape(n, nblk, 2 * i, 2 * ns)
    dcomp = jnp.transpose(cb.reshape(n, nblk, gl, i, i), (0, 1, 3, 2, 4)).reshape(n, nblk, i, gl * i)
    a2_re = (a_re * a_re - a_im * a_im).reshape(n, nblk, 1, ns)
    a2_im = (2.0 * a_re * a_im).reshape(n, nblk, 1, ns)
    return bcomp, ccomp, dcomp, a2_re, a2_im, d_skip.reshape(n, 1, g * i)


def _s5_layer(x2d, bsz, seqlen, norm_g, w_in, prep, layer, w_glu, b_glu, w_out, final_gain):
    d_inner = w_out.shape[0]
    proj = _norm_proj(x2d, norm_g, w_in.astype(BF16))
    y = _ssm_core(proj.reshape(bsz, seqlen, 2 * d_inner), prep, layer)
    return _glu_out(y.reshape(bsz * seqlen, d_inner), proj, w_glu.astype(BF16), b_glu,
                    w_out.astype(BF16), x2d, final_gain)


def kernel(x, gla_norm, gla_w_in, gla_w_gate_up, gla_b_gate, gla_head_gain, gla_w_out,
           s5_norm, s5_w_in, s5_lam_re, s5_lam_im, s5_log_dt, s5_b_re, s5_b_im,
           s5_c_re, s5_c_im, s5_d, s5_w_glu, s5_b_glu, s5_w_out, final_norm):
    bsz, seqlen, d = x.shape
    depth = gla_w_in.shape[0] + s5_w_in.shape[0]
    prep = _s5_prep(s5_lam_re, s5_lam_im, s5_log_dt, s5_b_re, s5_b_im, s5_c_re, s5_c_im, s5_d)
    x2d = x.reshape(bsz * seqlen, d)
    for i in range(depth):
        j = i // 2
        if i % 2 == 0:
            x2d = _gla_layer(x2d, bsz, seqlen, gla_norm[j], gla_w_in[j], gla_w_gate_up[j],
                             gla_b_gate[j], gla_head_gain[j], gla_w_out[j])
        else:
            x2d = _s5_layer(x2d, bsz, seqlen, s5_norm[j], s5_w_in[j], prep, j,
                            s5_w_glu[j], s5_b_glu[j], s5_w_out[j],
                            final_norm if i == depth - 1 else None)
    if depth % 2 == 1:
        x2d = _final_norm(x2d, final_norm)
    return x2d.reshape(bsz, seqlen, d)
```

```python
import functools
import math

import jax
import jax.numpy as jnp
from jax import lax
from jax.experimental import pallas as pl
from jax.experimental.pallas import tpu as pltpu

F32 = jnp.float32
BF16 = jnp.bfloat16

EPS = 1e-6
GLA_HEADS = 4
GLA_GATE_TAU = 16.0
GLA_CHUNK = 64
S5_GROUP = 16
S5_STATE = 64

LANES = 128
SUBLANES = 8
VMEM_LIMIT_BYTES = 56 * 1024 * 1024

ROW_TILE = 256
GLA_TILE = 256
PROJ_SLAB = 256
SSM_TILE = 256
SSM_SUB_PAIRS = 32


def _params(*sem):
    return pltpu.CompilerParams(dimension_semantics=sem, vmem_limit_bytes=VMEM_LIMIT_BYTES)


def _mm(a, b):
    return jnp.dot(a, b, preferred_element_type=F32)


def _mm_nt(a, b):
    return lax.dot_general(a, b, (((1,), (1,)), ((), ())), preferred_element_type=F32)


def _mm_tn(a, b):
    return lax.dot_general(a, b, (((0,), (0,)), ((), ())), preferred_element_type=F32)


def _iota2(shape):
    return (lax.broadcasted_iota(jnp.int32, shape, 0), lax.broadcasted_iota(jnp.int32, shape, 1))


def _rms(x, g):
    return x * lax.rsqrt(jnp.mean(x * x, axis=-1, keepdims=True) + EPS) * g


def _norm_proj_kernel(x_ref, g_ref, w_ref, o_ref):
    h = _rms(x_ref[...], g_ref[...])
    o_ref[...] = _mm(h.astype(BF16), w_ref[...])


def _norm_proj(x2d, g, w_bf16):
    t, d = x2d.shape
    n = w_bf16.shape[1]
    return pl.pallas_call(
        _norm_proj_kernel,
        grid=(t // ROW_TILE,),
        in_specs=[pl.BlockSpec((ROW_TILE, d), lambda i: (i, 0)),
                  pl.BlockSpec((1, d), lambda i: (0, 0)),
                  pl.BlockSpec((d, n), lambda i: (0, 0))],
        out_specs=pl.BlockSpec((ROW_TILE, n), lambda i: (i, 0)),
        out_shape=jax.ShapeDtypeStruct((t, n), F32),
        compiler_params=_params("parallel"),
        name="norm_proj",
    )(x2d, g.reshape(1, d), w_bf16)


def _final_norm_kernel(x_ref, g_ref, o_ref):
    o_ref[...] = _rms(x_ref[...], g_ref[...])


def _final_norm(x2d, g):
    t, d = x2d.shape
    return pl.pallas_call(
        _final_norm_kernel,
        grid=(t // ROW_TILE,),
        in_specs=[pl.BlockSpec((ROW_TILE, d), lambda i: (i, 0)),
                  pl.BlockSpec((1, d), lambda i: (0, 0))],
        out_specs=pl.BlockSpec((ROW_TILE, d), lambda i: (i, 0)),
        out_shape=jax.ShapeDtypeStruct((t, d), F32),
        compiler_params=_params("parallel"),
        name="final_norm",
    )(x2d, g.reshape(1, d))


def _split3_bf16(x):
    hi = x.astype(BF16)
    r1 = x - hi.astype(F32)
    mid = r1.astype(BF16)
    lo = (r1 - mid.astype(F32)).astype(BF16)
    return hi, mid, lo


def _gla_project_pieces(x_ref, g_ref, win_ref, wr_ref, wup_ref, bg_ref, h_ref, proj_ref, bcum_ref):
    tl = x_ref.shape[0]
    c = GLA_CHUNK

    def norm():
        h_ref[...] = _rms(x_ref[...], g_ref[...]).astype(BF16)

    def slab(j):
        cols = slice(j * PROJ_SLAB, (j + 1) * PROJ_SLAB)

        def run():
            proj_ref[:, cols] = _mm(h_ref[...], win_ref[:, cols])
        return run

    def decay():
        gate = _mm(_mm(h_ref[...], wr_ref[...]).astype(BF16), wup_ref[...]) + bg_ref[...]
        log_a = jax.nn.log_sigmoid(gate) / GLA_GATE_TAU
        row, col = _iota2((tl, tl))
        ltri = jnp.where((row // c == col // c) & (col <= row), 1.0, 0.0).astype(BF16)
        hi, mid, lo = _split3_bf16(log_a)
        bcum_ref[...] = _mm(ltri, hi) + _mm(ltri, mid) + _mm(ltri, lo)

    return [norm] + [slab(j) for j in range(proj_ref.shape[-1] // PROJ_SLAB)] + [decay]


def _gla_mix(x_ref, gain_ref, wout_ref, o_ref, proj_ref, bcum_ref, og_ref, s_ref, first, other,
             *, heads, dk, dv):
    tl = x_ref.shape[0]
    c = GLA_CHUNK
    kd = heads * dk
    di = heads * dv
    q0, k0, v0, z0 = 0, kd, 2 * kd, 2 * kd + di

    crow, ccol = _iota2((c, c))
    causal = ccol <= crow
    drow, dcol = _iota2((dk, dk))
    eye = drow == dcol
    scale = dk ** -0.5
    gain = gain_ref[...]

    n_units = (tl // c) * heads
    other[0]()
    for ci in range(tl // c):
        sl = slice(ci * c, (ci + 1) * c)
        for hh in range(heads):
            unit = ci * heads + hh
            for piece in other[1 + unit * (len(other) - 1) // n_units:
                               1 + (unit + 1) * (len(other) - 1) // n_units]:
                piece()
            bc = bcum_ref[sl, hh * dk:(hh + 1) * dk]
            bl = bc[c - 1:c, :]
            q = proj_ref[sl, q0 + hh * dk:q0 + (hh + 1) * dk]
            k = proj_ref[sl, k0 + hh * dk:k0 + (hh + 1) * dk]
            qg = (q * scale) * jnp.exp(bc)
            kg = k * jnp.exp(-bc)
            ke = k * jnp.exp(bl - bc)
            vb = proj_ref[sl, v0 + hh * dv:v0 + (hh + 1) * dv].astype(BF16)
            qgb = qg.astype(BF16)

            att = jnp.where(causal, _mm_nt(qgb, kg.astype(BF16)), 0.0)
            s_prev = s_ref[hh]
            if ci == 0:
                s_prev = jnp.where(first, 0.0, s_prev)
            o = _mm(jnp.concatenate([qgb, att.astype(BF16)], axis=1),
                    jnp.concatenate([s_prev.astype(BF16), vb], axis=0))

            dec = jnp.exp(bl)
            dec_col = jnp.sum(jnp.where(eye, jnp.broadcast_to(dec, (dk, dk)), 0.0),
                              axis=1, keepdims=True)
            s_ref[hh] = dec_col * s_prev + _mm_tn(ke.astype(BF16), vb)

            z = proj_ref[sl, z0 + hh * dv:z0 + (hh + 1) * dv]
            og_ref[sl, hh * dv:(hh + 1) * dv] = (_rms(o, gain) * jax.nn.silu(z)).astype(BF16)

    o_ref[...] = x_ref[...] + _mm(og_ref[...], wout_ref[...])


def _gla_layer_kernel(xa_ref, xb_ref, g_ref, win_ref, wr_ref, wup_ref, bg_ref, gain_ref, wout_ref,
                      o_ref, h_ref, proj_ref, bcum_ref, og_ref, s_ref, *, nt, heads, dk, dv):
    s = pl.program_id(0)

    @pl.when(s == 0)
    def _():
        proj_ref[1] = jnp.zeros(proj_ref.shape[1:], F32)
        bcum_ref[1] = jnp.zeros(bcum_ref.shape[1:], F32)
        s_ref[...] = jnp.zeros_like(s_ref)

    first = lax.rem(s + (nt - 1), nt) == 0
    even = lax.rem(s, 2) == 0

    def step(wr_slot, rd_slot):
        pieces = _gla_project_pieces(xa_ref, g_ref, win_ref, wr_ref, wup_ref, bg_ref, h_ref,
                                     proj_ref.at[wr_slot], bcum_ref.at[wr_slot])
        _gla_mix(xb_ref, gain_ref, wout_ref, o_ref, proj_ref.at[rd_slot], bcum_ref.at[rd_slot],
                 og_ref, s_ref, first, pieces, heads=heads, dk=dk, dv=dv)

    @pl.when(even)
    def _():
        step(0, 1)

    @pl.when(jnp.logical_not(even))
    def _():
        step(1, 0)


def _resident(shape):
    return pl.BlockSpec(shape, lambda *_: (0,) * len(shape), pipeline_mode=pl.Buffered(1))


def _gla_layer(x2d, bsz, seqlen, norm_g, w_in, w_gate_up, b_gate, head_gain, w_out):
    t, d = x2d.shape
    rank, kd = w_gate_up.shape
    d_inner = w_out.shape[0]
    dk = kd // GLA_HEADS
    dv = d_inner // GLA_HEADS
    n_proj = 2 * kd + 2 * d_inner
    nt = seqlen // GLA_TILE
    n_tiles = bsz * nt
    w_main = w_in[:, :n_proj].astype(BF16)
    w_r_pad = jnp.concatenate(
        [w_in[:, n_proj:], jnp.zeros((d, LANES - rank), w_in.dtype)], axis=1).astype(BF16)
    w_up_pad = jnp.concatenate(
        [w_gate_up, jnp.zeros((LANES - rank, kd), w_gate_up.dtype)], axis=0).astype(BF16)
    return pl.pallas_call(
        functools.partial(_gla_layer_kernel, nt=nt, heads=GLA_HEADS, dk=dk, dv=dv),
        grid=(n_tiles + 1,),
        in_specs=[pl.BlockSpec((GLA_TILE, d), lambda s: (jnp.minimum(s, n_tiles - 1), 0)),
                  pl.BlockSpec((GLA_TILE, d), lambda s: (jnp.maximum(s - 1, 0), 0)),
                  _resident((1, d)),
                  _resident((d, n_proj)),
                  _resident((d, LANES)),
                  _resident((LANES, kd)),
                  _resident((1, kd)),
                  _resident((1, dv)),
                  _resident((d_inner, d))],
        out_specs=pl.BlockSpec((GLA_TILE, d), lambda s: (jnp.maximum(s - 1, 0), 0)),
        out_shape=jax.ShapeDtypeStruct((t, d), F32),
        scratch_shapes=[pltpu.VMEM((GLA_TILE, d), BF16),
                        pltpu.VMEM((2, GLA_TILE, n_proj), F32),
                        pltpu.VMEM((2, GLA_TILE, kd), F32),
                        pltpu.VMEM((GLA_TILE, d_inner), BF16),
                        pltpu.VMEM((GLA_HEADS, dk, dv), F32)],
        compiler_params=_params("arbitrary"),
        name="gla_layer",
    )(x2d, x2d, norm_g.reshape(1, d), w_main, w_r_pad, w_up_pad, b_gate.reshape(1, kd),
      head_gain.reshape(1, dv), w_out.astype(BF16))


def _ssm_expand(bc_ref, cc_ref, dc_ref, bm_ref, cm_ref, cb_ref):
    gi = dc_ref.shape[2]
    ns = bc_ref.shape[3] // 2
    gstates = ns // (LANES // gi)
    bc = bc_ref[0, 0].astype(BF16)
    cc = cc_ref[0, 0].astype(BF16)
    dc = dc_ref[0, 0].astype(BF16)

    r, c = _iota2((2 * LANES, 2 * gi))
    sel = jnp.where((r // LANES == c // gi) & (r % gi == c % gi), 1.0, 0.0).astype(BF16)
    r, c = _iota2((2 * LANES, 2 * ns))
    keep = (r % LANES) // gi == (c % ns) // gstates
    bm_ref[...] = jnp.where(keep, _mm(sel, bc), 0.0).astype(BF16)

    r, c = _iota2((2 * gi, 2 * LANES))
    sel_t = jnp.where((c // LANES == r // gi) & (c % gi == r % gi), 1.0, 0.0).astype(BF16)
    r, c = _iota2((2 * ns, 2 * LANES))
    keep = (r % ns) // gstates == (c % LANES) // gi
    cm_ref[...] = jnp.where(keep, _mm_tn(cc, sel_t), 0.0).astype(BF16)

    r, c = _iota2((gi, LANES))
    sel_o = jnp.where(c % gi == r, 1.0, 0.0).astype(BF16)
    r, c = _iota2((LANES, LANES))
    cb_ref[...] = jnp.where(r // gi == c // gi, _mm_tn(dc, sel_o), 0.0).astype(BF16)


def _ssm_kernel(u_ref, bc_ref, cc_ref, dc_ref, a2re_ref, a2im_ref, d_ref, y_ref,
                bm_ref, cm_ref, cb_ref, ue_ref, uo_ref, u2_ref, w_ref, tb_ref, st_ref, zsh_ref,
                ye_ref, yo_ref):
    nb, tl, lanes = u_ref.shape
    npair = tl // 2
    rows = npair * nb
    ns = st_ref.shape[1] // 2
    sub = SSM_SUB_PAIRS
    sub_rows = sub * nb

    @pl.when(pl.program_id(1) == 0)
    def _():
        st_ref[...] = jnp.zeros_like(st_ref)
        zsh_ref[0:nb, :] = jnp.zeros((nb, lanes), F32)
        _ssm_expand(bc_ref, cc_ref, dc_ref, bm_ref, cm_ref, cb_ref)

    for b in range(nb):
        ue_ref[pl.ds(b, npair, stride=nb), :] = u_ref[b, pl.ds(0, npair, stride=2), :]
        uo_ref[pl.ds(b, npair, stride=nb), :] = u_ref[b, pl.ds(1, npair, stride=2), :]

    u2_ref[...] = jnp.concatenate([ue_ref[...], uo_ref[...]], axis=1).astype(BF16)

    a_re = jnp.broadcast_to(a2re_ref[0, 0], (nb, ns))
    a_im = jnp.broadcast_to(a2im_ref[0, 0], (nb, ns))
    d = d_ref[0]
    s_re = st_ref[:, 0:ns]
    s_im = st_ref[:, ns:2 * ns]

    for k in range(npair // sub):
        blk = slice(k * sub_rows, (k + 1) * sub_rows)
        w_ref[blk, :] = _mm(u2_ref[blk, :], bm_ref[...])
        for m in range(0, sub, 2):
            r0 = k * sub_rows + m * nb
            m_re = a_re * s_re - a_im * s_im + w_ref[r0:r0 + nb, 0:ns]
            m_im = a_re * s_im + a_im * s_re + w_ref[r0:r0 + nb, ns:2 * ns]
            s_re = a_re * m_re - a_im * m_im + w_ref[r0 + nb:r0 + 2 * nb, 0:ns]
            s_im = a_re * m_im + a_im * m_re + w_ref[r0 + nb:r0 + 2 * nb, ns:2 * ns]
            tb_ref[r0:r0 + 2 * nb, 0:ns] = jnp.concatenate([m_re, s_re], axis=0).astype(BF16)
            tb_ref[r0:r0 + 2 * nb, ns:2 * ns] = jnp.concatenate([m_im, s_im], axis=0).astype(BF16)
        z = _mm(tb_ref[blk, :], cm_ref[...])
        zsh_ref[nb + k * sub_rows:nb + (k + 1) * sub_rows, :] = z[:, 0:lanes]
        ye_ref[blk, :] = (zsh_ref[blk, :] + _mm(u2_ref[blk, 0:lanes], cb_ref[...])
                          + d * ue_ref[blk, :])
        yo_ref[blk, :] = z[:, lanes:2 * lanes] + d * uo_ref[blk, :]

    st_ref[:, 0:ns] = s_re
    st_ref[:, ns:2 * ns] = s_im
    zsh_ref[0:nb, :] = zsh_ref[rows:rows + nb, :]
    for b in range(nb):
        y_ref[b, pl.ds(0, npair, stride=2), :] = ye_ref[pl.ds(b, npair, stride=nb), :]
        y_ref[b, pl.ds(1, npair, stride=2), :] = yo_ref[pl.ds(b, npair, stride=nb), :]


def _ssm_core(proj3, prep, layer):
    bcomp, ccomp, dcomp, a2_re, a2_im, d_row = prep
    bsz, seqlen, _ = proj3.shape
    _, nblk, gi2, two_ns = bcomp.shape
    d_inner = nblk * LANES
    tl = SSM_TILE
    rows = (tl // 2) * bsz
    per_block = lambda *blk: pl.BlockSpec((1, 1) + blk, lambda j, i: (layer, j, 0, 0))
    return pl.pallas_call(
        _ssm_kernel,
        grid=(nblk, seqlen // tl),
        in_specs=[
            pl.BlockSpec((bsz, tl, LANES), lambda j, i: (0, i, j)),
            per_block(gi2, two_ns),
            per_block(gi2, two_ns),
            per_block(gi2 // 2, LANES),
            per_block(1, two_ns // 2),
            per_block(1, two_ns // 2),
            pl.BlockSpec((1, 1, LANES), lambda j, i: (layer, 0, j)),
        ],
        out_specs=pl.BlockSpec((bsz, tl, LANES), lambda j, i: (0, i, j)),
        out_shape=jax.ShapeDtypeStruct((bsz, seqlen, d_inner), F32),
        scratch_shapes=[pltpu.VMEM((2 * LANES, two_ns), BF16),
                        pltpu.VMEM((two_ns, 2 * LANES), BF16),
                        pltpu.VMEM((LANES, LANES), BF16),
                        pltpu.VMEM((rows, LANES), F32),
                        pltpu.VMEM((rows, LANES), F32),
                        pltpu.VMEM((rows, 2 * LANES), BF16),
                        pltpu.VMEM((rows, two_ns), F32),
                        pltpu.VMEM((rows, two_ns), BF16),
                        pltpu.VMEM((bsz, two_ns), F32),
                        pltpu.VMEM((rows + bsz, LANES), F32),
                        pltpu.VMEM((rows, LANES), F32),
                        pltpu.VMEM((rows, LANES), F32)],
        compiler_params=_params("parallel", "arbitrary"),
        name="ssm_core",
    )(proj3, bcomp, ccomp, dcomp, a2_re, a2_im, d_row)


def _glu_out_kernel(y_ref, z_ref, wg_ref, bg_ref, wo_ref, x_ref, *rest):
    o_ref = rest[-1]
    y = jax.nn.gelu(y_ref[...])
    y = y * jax.nn.sigmoid(_mm(y.astype(BF16), wg_ref[...]) + bg_ref[...])
    y = y * jax.nn.silu(z_ref[...])
    out = x_ref[...] + _mm(y.astype(BF16), wo_ref[...])
    if len(rest) == 2:
        out = _rms(out, rest[0][...])
    o_ref[...] = out


def _glu_out(y2d, proj, w_glu, b_glu, w_out, x2d, final_gain=None):
    t, di = y2d.shape
    d = x2d.shape[1]
    in_specs = [pl.BlockSpec((ROW_TILE, di), lambda i: (i, 0)),
                pl.BlockSpec((ROW_TILE, di), lambda i: (i, 1)),
                _resident((di, di)),
                _resident((1, di)),
                _resident((di, d)),
                pl.BlockSpec((ROW_TILE, d), lambda i: (i, 0))]
    args = [y2d, proj, w_glu, b_glu.reshape(1, di), w_out, x2d]
    if final_gain is not None:
        in_specs.append(_resident((1, d)))
        args.append(final_gain.reshape(1, d))
    return pl.pallas_call(
        _glu_out_kernel,
        grid=(t // ROW_TILE,),
        in_specs=in_specs,
        out_specs=pl.BlockSpec((ROW_TILE, d), lambda i: (i, 0)),
        out_shape=jax.ShapeDtypeStruct((t, d), F32),
        compiler_params=_params("parallel"),
        name="glu_out",
    )(*args)


def _s5_prep(lam_re, lam_im, log_dt, b_re, b_im, c_re, c_im, d_skip):
    n, g, p = lam_re.shape
    i = b_re.shape[-1]
    gl = LANES // i
    nblk = g // gl
    ns = gl * p
    dt = jnp.exp(log_dt)[..., None]
    mag = jnp.exp(lam_re * dt)
    a_re = mag * jnp.cos(lam_im * dt)
    a_im = mag * jnp.sin(lam_im * dt)
    den = lam_re * lam_re + lam_im * lam_im
    bc_re = ((a_re - 1.0) * lam_re + a_im * lam_im) / den
    bc_im = (a_im * lam_re - (a_re - 1.0) * lam_im) / den
    bb_re = bc_re[..., None] * b_re - bc_im[..., None] * b_im
    bb_im = bc_re[..., None] * b_im + bc_im[..., None] * b_re
    ab_re = a_re[..., None] * bb_re - a_im[..., None] * bb_im
    ab_im = a_re[..., None] * bb_im + a_im[..., None] * bb_re
    ca_re = c_re * a_re[:, :, None, :] - c_im * a_im[:, :, None, :]
    ca_im = c_re * a_im[:, :, None, :] + c_im * a_re[:, :, None, :]
    cb = (jnp.einsum('ngop,ngpi->ngoi', c_re, bb_re, precision=lax.Precision.HIGHEST)
          - jnp.einsum('ngop,ngpi->ngoi', c_im, bb_im, precision=lax.Precision.HIGHEST))

    bsrc = jnp.stack([ab_re, ab_im, bb_re, bb_im], axis=1).reshape(n, 2, 2, nblk, gl, p, i)
    bcomp = jnp.transpose(bsrc, (0, 3, 1, 6, 2, 4, 5)).reshape(n, nblk, 2 * i, 2 * ns)
    csrc = jnp.stack([ca_re, -ca_im, c_re, -c_im], axis=1).reshape(n, 2, 2, nblk, gl, i, p)
    ccomp = jnp.transpose(csrc, (0, 3, 1, 5, 2, 4, 6)).reshape(n, nblk, 2 * i, 2 * ns)
    dcomp = jnp.transpose(cb.reshape(n, nblk, gl, i, i), (0, 1, 3, 2, 4)).reshape(n, nblk, i, gl * i)
    a2_re = (a_re * a_re - a_im * a_im).reshape(n, nblk, 1, ns)
    a2_im = (2.0 * a_re * a_im).reshape(n, nblk, 1, ns)
    return bcomp, ccomp, dcomp, a2_re, a2_im, d_skip.reshape(n, 1, g * i)


def _s5_layer(x2d, bsz, seqlen, norm_g, w_in, prep, layer, w_glu, b_glu, w_out, final_gain):
    d_inner = w_out.shape[0]
    proj = _norm_proj(x2d, norm_g, w_in.astype(BF16))
    y = _ssm_core(proj.reshape(bsz, seqlen, 2 * d_inner), prep, layer)
    return _glu_out(y.reshape(bsz * seqlen, d_inner), proj, w_glu.astype(BF16), b_glu,
                    w_out.astype(BF16), x2d, final_gain)


def kernel(x, gla_norm, gla_w_in, gla_w_gate_up, gla_b_gate, gla_head_gain, gla_w_out,
           s5_norm, s5_w_in, s5_lam_re, s5_lam_im, s5_log_dt, s5_b_re, s5_b_im,
           s5_c_re, s5_c_im, s5_d, s5_w_glu, s5_b_glu, s5_w_out, final_norm):
    bsz, seqlen, d = x.shape
    depth = gla_w_in.shape[0] + s5_w_in.shape[0]
    prep = _s5_prep(s5_lam_re, s5_lam_im, s5_log_dt, s5_b_re, s5_b_im, s5_c_re, s5_c_im, s5_d)
    x2d = x.reshape(bsz * seqlen, d)
    for i in range(depth):
        j = i // 2
        if i % 2 == 0:
            x2d = _gla_layer(x2d, bsz, seqlen, gla_norm[j], gla_w_in[j], gla_w_gate_up[j],
                             gla_b_gate[j], gla_head_gain[j], gla_w_out[j])
        else:
            x2d = _s5_layer(x2d, bsz, seqlen, s5_norm[j], s5_w_in[j], prep, j,
                            s5_w_glu[j], s5_b_glu[j], s5_w_out[j],
                            final_norm if i == depth - 1 else None)
    if depth % 2 == 1:
        x2d = _final_norm(x2d, final_norm)
    return x2d.reshape(bsz, seqlen, d)
```

```python
import functools

import jax
import jax.numpy as jnp
from jax import lax
from jax.experimental import pallas as pl
from jax.experimental.pallas import tpu as pltpu

F32 = jnp.float32
BF16 = jnp.bfloat16

EPS = 1e-6
GLA_HEADS = 4
GLA_GATE_TAU = 16.0
GLA_CHUNK = 64

LANES = 128
VMEM_LIMIT_BYTES = 56 * 1024 * 1024

ROW_TILE = 512
GLA_TILE = 256
PROJ_SLAB = 256
SSM_TILE = 1024
SSM_SUB_PAIRS = 32


def _params(*sem):
    return pltpu.CompilerParams(dimension_semantics=sem, vmem_limit_bytes=VMEM_LIMIT_BYTES)


def _mm(a, b):
    return jnp.dot(a, b, preferred_element_type=F32)


def _mm_nt(a, b):
    return lax.dot_general(a, b, (((1,), (1,)), ((), ())), preferred_element_type=F32)


def _mm_tn(a, b):
    return lax.dot_general(a, b, (((0,), (0,)), ((), ())), preferred_element_type=F32)


def _iota2(shape):
    return (lax.broadcasted_iota(jnp.int32, shape, 0), lax.broadcasted_iota(jnp.int32, shape, 1))


def _rms(x, g):
    return x * lax.rsqrt(jnp.mean(x * x, axis=-1, keepdims=True) + EPS) * g


def _resident(shape):
    return pl.BlockSpec(shape, lambda *_: (0,) * len(shape), pipeline_mode=pl.Buffered(1))


def _to_time_major(x_ref, slab_ref):
    nb, tl, d = x_ref.shape
    for b in range(nb):
        for c in range(d // LANES):
            slab_ref[c, pl.ds(b, tl, stride=nb), :] = x_ref[b, :, c * LANES:(c + 1) * LANES]
    return jnp.concatenate([slab_ref[c] for c in range(d // LANES)], axis=1)


def _norm_proj_kernel(x_ref, g_ref, w_ref, o_ref, slab_ref):
    h = _rms(_to_time_major(x_ref, slab_ref), g_ref[...])
    o_ref[...] = _mm(h.astype(BF16), w_ref[...])


def _norm_proj(x3d, g, w_bf16):
    bsz, seqlen, d = x3d.shape
    n = w_bf16.shape[1]
    tl = ROW_TILE // bsz
    return pl.pallas_call(
        _norm_proj_kernel,
        grid=(seqlen // tl,),
        in_specs=[pl.BlockSpec((bsz, tl, d), lambda i: (0, i, 0)),
                  _resident((1, d)),
                  _resident((d, n))],
        out_specs=pl.BlockSpec((ROW_TILE, n), lambda i: (i, 0)),
        out_shape=jax.ShapeDtypeStruct((seqlen * bsz, n), F32),
        scratch_shapes=[pltpu.VMEM((d // LANES, ROW_TILE, LANES), F32)],
        compiler_params=_params("parallel"),
        name="norm_proj",
    )(x3d, g.reshape(1, d), w_bf16)


def _final_norm_kernel(x_ref, g_ref, o_ref):
    o_ref[...] = _rms(x_ref[...], g_ref[...])


def _final_norm(x2d, g):
    t, d = x2d.shape
    return pl.pallas_call(
        _final_norm_kernel,
        grid=(t // ROW_TILE,),
        in_specs=[pl.BlockSpec((ROW_TILE, d), lambda i: (i, 0)),
                  pl.BlockSpec((1, d), lambda i: (0, 0))],
        out_specs=pl.BlockSpec((ROW_TILE, d), lambda i: (i, 0)),
        out_shape=jax.ShapeDtypeStruct((t, d), F32),
        compiler_params=_params("parallel"),
        name="final_norm",
    )(x2d, g.reshape(1, d))


def _split3_bf16(x):
    hi = x.astype(BF16)
    r1 = x - hi.astype(F32)
    mid = r1.astype(BF16)
    lo = (r1 - mid.astype(F32)).astype(BF16)
    return hi, mid, lo


def _gla_project_pieces(x_ref, g_ref, win_ref, wr_ref, wup_ref, bg_ref, h_ref, proj_ref, bcum_ref):
    tl = x_ref.shape[0]
    c = GLA_CHUNK

    def norm():
        h_ref[...] = _rms(x_ref[...], g_ref[...]).astype(BF16)

    def slab(j):
        cols = slice(j * PROJ_SLAB, (j + 1) * PROJ_SLAB)

        def run():
            proj_ref[:, cols] = _mm(h_ref[...], win_ref[:, cols])
        return run

    def decay():
        gate = _mm(_mm(h_ref[...], wr_ref[...]).astype(BF16), wup_ref[...]) + bg_ref[...]
        log_a = jax.nn.log_sigmoid(gate) / GLA_GATE_TAU
        row, col = _iota2((tl, tl))
        ltri = jnp.where((row // c == col // c) & (col <= row), 1.0, 0.0).astype(BF16)
        hi, mid, lo = _split3_bf16(log_a)
        bcum_ref[...] = _mm(ltri, hi) + _mm(ltri, mid) + _mm(ltri, lo)

    return [norm] + [slab(j) for j in range(proj_ref.shape[-1] // PROJ_SLAB)] + [decay]


def _gla_mix(x_ref, gain_ref, wout_ref, o_ref, proj_ref, bcum_ref, og_ref, s_ref, first, other,
             *, heads, dk, dv):
    tl = x_ref.shape[0]
    c = GLA_CHUNK
    kd = heads * dk
    di = heads * dv
    q0, k0, v0, z0 = 0, kd, 2 * kd, 2 * kd + di

    crow, ccol = _iota2((c, c))
    causal = ccol <= crow
    drow, dcol = _iota2((dk, dk))
    eye = drow == dcol
    scale = dk ** -0.5
    gain = gain_ref[...]

    n_slots = 2 * (tl // c) * heads

    def emit_other(slot):
        for piece in other[1 + slot * (len(other) - 1) // n_slots:
                           1 + (slot + 1) * (len(other) - 1) // n_slots]:
            piece()

    other[0]()
    for ci in range(tl // c):
        sl = slice(ci * c, (ci + 1) * c)
        for hh in range(heads):
            unit = ci * heads + hh
            emit_other(2 * unit)
            bc = bcum_ref[sl, hh * dk:(hh + 1) * dk]
            bl = bc[c - 1:c, :]
            q = proj_ref[sl, q0 + hh * dk:q0 + (hh + 1) * dk]
            k = proj_ref[sl, k0 + hh * dk:k0 + (hh + 1) * dk]
            qg = (q * scale) * jnp.exp(bc)
            kg = k * jnp.exp(-bc)
            ke = k * jnp.exp(bl - bc)
            vb = proj_ref[sl, v0 + hh * dv:v0 + (hh + 1) * dv].astype(BF16)
            qgb = qg.astype(BF16)

            att = jnp.where(causal, _mm_nt(qgb, kg.astype(BF16)), 0.0)
            emit_other(2 * unit + 1)
            s_prev = s_ref[hh]
            if ci == 0:
                s_prev = jnp.where(first, 0.0, s_prev)
            o = _mm(jnp.concatenate([qgb, att.astype(BF16)], axis=1),
                    jnp.concatenate([s_prev.astype(BF16), vb], axis=0))

            dec = jnp.exp(bl)
            dec_col = jnp.sum(jnp.where(eye, jnp.broadcast_to(dec, (dk, dk)), 0.0),
                              axis=1, keepdims=True)
            s_ref[hh] = dec_col * s_prev + _mm_tn(ke.astype(BF16), vb)

            z = proj_ref[sl, z0 + hh * dv:z0 + (hh + 1) * dv]
            og_ref[sl, hh * dv:(hh + 1) * dv] = (_rms(o, gain) * jax.nn.silu(z)).astype(BF16)

    o_ref[...] = x_ref[...] + _mm(og_ref[...], wout_ref[...])


def _gla_layer_kernel(xa_ref, xb_ref, g_ref, win_ref, wr_ref, wup_ref, bg_ref, gain_ref, wout_ref,
                      o_ref, h_ref, proj_ref, bcum_ref, og_ref, s_ref, *, nt, heads, dk, dv):
    s = pl.program_id(0)

    @pl.when(s == 0)
    def _():
        proj_ref[1] = jnp.zeros(proj_ref.shape[1:], F32)
        bcum_ref[1] = jnp.zeros(bcum_ref.shape[1:], F32)
        s_ref[...] = jnp.zeros_like(s_ref)

    first = lax.rem(s + (nt - 1), nt) == 0
    even = lax.rem(s, 2) == 0

    def step(wr_slot, rd_slot):
        pieces = _gla_project_pieces(xa_ref, g_ref, win_ref, wr_ref, wup_ref, bg_ref, h_ref,
                                     proj_ref.at[wr_slot], bcum_ref.at[wr_slot])
        _gla_mix(xb_ref, gain_ref, wout_ref, o_ref, proj_ref.at[rd_slot], bcum_ref.at[rd_slot],
                 og_ref, s_ref, first, pieces, heads=heads, dk=dk, dv=dv)

    @pl.when(even)
    def _():
        step(0, 1)

    @pl.when(jnp.logical_not(even))
    def _():
        step(1, 0)


def _gla_layer(x2d, bsz, seqlen, norm_g, w_in, w_gate_up, b_gate, head_gain, w_out):
    t, d = x2d.shape
    rank, kd = w_gate_up.shape
    d_inner = w_out.shape[0]
    dk = kd // GLA_HEADS
    dv = d_inner // GLA_HEADS
    n_proj = 2 * kd + 2 * d_inner
    nt = seqlen // GLA_TILE
    n_tiles = bsz * nt
    w_main = w_in[:, :n_proj].astype(BF16)
    w_r_pad = jnp.concatenate(
        [w_in[:, n_proj:], jnp.zeros((d, LANES - rank), w_in.dtype)], axis=1).astype(BF16)
    w_up_pad = jnp.concatenate(
        [w_gate_up, jnp.zeros((LANES - rank, kd), w_gate_up.dtype)], axis=0).astype(BF16)
    return pl.pallas_call(
        functools.partial(_gla_layer_kernel, nt=nt, heads=GLA_HEADS, dk=dk, dv=dv),
        grid=(n_tiles + 1,),
        in_specs=[pl.BlockSpec((GLA_TILE, d), lambda s: (jnp.minimum(s, n_tiles - 1), 0)),
                  pl.BlockSpec((GLA_TILE, d), lambda s: (jnp.maximum(s - 1, 0), 0)),
                  _resident((1, d)),
                  _resident((d, n_proj)),
                  _resident((d, LANES)),
                  _resident((LANES, kd)),
                  _resident((1, kd)),
                  _resident((1, dv)),
                  _resident((d_inner, d))],
        out_specs=pl.BlockSpec((GLA_TILE, d), lambda s: (jnp.maximum(s - 1, 0), 0)),
        out_shape=jax.ShapeDtypeStruct((t, d), F32),
        scratch_shapes=[pltpu.VMEM((GLA_TILE, d), BF16),
                        pltpu.VMEM((2, GLA_TILE, n_proj), F32),
                        pltpu.VMEM((2, GLA_TILE, kd), F32),
                        pltpu.VMEM((GLA_TILE, d_inner), BF16),
                        pltpu.VMEM((GLA_HEADS, dk, dv), F32)],
        compiler_params=_params("arbitrary"),
        name="gla_layer",
    )(x2d, x2d, norm_g.reshape(1, d), w_main, w_r_pad, w_up_pad, b_gate.reshape(1, kd),
      head_gain.reshape(1, dv), w_out.astype(BF16))


def _ssm_expand(bc_ref, cc_ref, dc_ref, bm_ref, cm_ref, cb_ref):
    gi = dc_ref.shape[2]
    ns = bc_ref.shape[3] // 2
    gstates = ns // (LANES // gi)
    bc = bc_ref[0, 0].astype(BF16)
    cc = cc_ref[0, 0].astype(BF16)
    dc = dc_ref[0, 0].astype(BF16)

    r, c = _iota2((2 * LANES, 2 * gi))
    sel = jnp.where((r // LANES == c // gi) & (r % gi == c % gi), 1.0, 0.0).astype(BF16)
    r, c = _iota2((2 * LANES, 2 * ns))
    keep = (r % LANES) // gi == (c % ns) // gstates
    bm_ref[...] = jnp.where(keep, _mm(sel, bc), 0.0).astype(BF16)

    r, c = _iota2((2 * gi, 2 * LANES))
    sel_t = jnp.where((c // LANES == r // gi) & (c % gi == r % gi), 1.0, 0.0).astype(BF16)
    r, c = _iota2((2 * ns, 2 * LANES))
    keep = (r % ns) // gstates == (c % LANES) // gi
    cm_ref[...] = jnp.where(keep, _mm_tn(cc, sel_t), 0.0).astype(BF16)

    r, c = _iota2((gi, LANES))
    sel_o = jnp.where(c % gi == r, 1.0, 0.0).astype(BF16)
    r, c = _iota2((LANES, LANES))
    cb_ref[...] = jnp.where(r // gi == c // gi, _mm_tn(dc, sel_o), 0.0).astype(BF16)


def _ssm_kernel(u_ref, bc_ref, cc_ref, dc_ref, a2re_ref, a2im_ref, d_ref, y_ref,
                bm_ref, cm_ref, cb_ref, u2_ref, w_ref, tb_ref, st_ref, zsh_ref, *, nb):
    lanes = u_ref.shape[1]
    npair = u_ref.shape[0] // (2 * nb)
    rows = npair * nb
    ns = st_ref.shape[1] // 2
    sub = SSM_SUB_PAIRS
    sub_rows = sub * nb

    @pl.when(pl.program_id(1) == 0)
    def _():
        st_ref[...] = jnp.zeros_like(st_ref)
        zsh_ref[0:nb, :] = jnp.zeros((nb, lanes), F32)
        _ssm_expand(bc_ref, cc_ref, dc_ref, bm_ref, cm_ref, cb_ref)

    def even_odd(k):
        u = u_ref[2 * k * sub_rows:2 * (k + 1) * sub_rows, :].reshape(sub, 2 * nb, lanes)
        return u[:, 0:nb, :].reshape(sub_rows, lanes), u[:, nb:2 * nb, :].reshape(sub_rows, lanes)

    for k in range(npair // sub):
        ue, uo = even_odd(k)
        u2_ref[k * sub_rows:(k + 1) * sub_rows, :] = jnp.concatenate([ue, uo], axis=1).astype(BF16)

    a_re = jnp.broadcast_to(a2re_ref[0, 0], (nb, ns))
    a_im = jnp.broadcast_to(a2im_ref[0, 0], (nb, ns))
    d = d_ref[0]
    s_re = st_ref[:, 0:ns]
    s_im = st_ref[:, ns:2 * ns]

    for k in range(npair // sub):
        blk = slice(k * sub_rows, (k + 1) * sub_rows)
        w_ref[blk, :] = _mm(u2_ref[blk, :], bm_ref[...])
        for m in range(0, sub, 2):
            r0 = k * sub_rows + m * nb
            m_re = a_re * s_re - a_im * s_im + w_ref[r0:r0 + nb, 0:ns]
            m_im = a_re * s_im + a_im * s_re + w_ref[r0:r0 + nb, ns:2 * ns]
            s_re = a_re * m_re - a_im * m_im + w_ref[r0 + nb:r0 + 2 * nb, 0:ns]
            s_im = a_re * m_im + a_im * m_re + w_ref[r0 + nb:r0 + 2 * nb, ns:2 * ns]
            tb_ref[r0:r0 + 2 * nb, 0:ns] = jnp.concatenate([m_re, s_re], axis=0).astype(BF16)
            tb_ref[r0:r0 + 2 * nb, ns:2 * ns] = jnp.concatenate([m_im, s_im], axis=0).astype(BF16)
        z = _mm(tb_ref[blk, :], cm_ref[...])
        zsh_ref[nb + k * sub_rows:nb + (k + 1) * sub_rows, :] = z[:, 0:lanes]
        ue, uo = even_odd(k)
        ye = zsh_ref[blk, :] + _mm(u2_ref[blk, 0:lanes], cb_ref[...]) + d * ue
        yo = z[:, lanes:2 * lanes] + d * uo
        y_ref[2 * k * sub_rows:2 * (k + 1) * sub_rows, :] = jnp.concatenate(
            [ye.reshape(sub, nb, lanes), yo.reshape(sub, nb, lanes)], axis=1).reshape(2 * sub_rows, lanes)

    st_ref[:, 0:ns] = s_re
    st_ref[:, ns:2 * ns] = s_im
    zsh_ref[0:nb, :] = zsh_ref[rows:rows + nb, :]


def _ssm_core(proj, prep, layer, bsz):
    bcomp, ccomp, dcomp, a2_re, a2_im, d_row = prep
    _, nblk, gi2, two_ns = bcomp.shape
    d_inner = nblk * LANES
    tl = SSM_TILE
    rows = (tl // 2) * bsz
    per_block = lambda *blk: pl.BlockSpec((1, 1) + blk, lambda j, i: (layer, j, 0, 0))
    return pl.pallas_call(
        functools.partial(_ssm_kernel, nb=bsz),
        grid=(nblk, proj.shape[0] // (tl * bsz)),
        in_specs=[
            pl.BlockSpec((tl * bsz, LANES), lambda j, i: (i, j)),
            per_block(gi2, two_ns),
            per_block(gi2, two_ns),
            per_block(gi2 // 2, LANES),
            per_block(1, two_ns // 2),
            per_block(1, two_ns // 2),
            pl.BlockSpec((1, 1, LANES), lambda j, i: (layer, 0, j)),
        ],
        out_specs=pl.BlockSpec((tl * bsz, LANES), lambda j, i: (i, j)),
        out_shape=jax.ShapeDtypeStruct((proj.shape[0], d_inner), F32),
        scratch_shapes=[pltpu.VMEM((2 * LANES, two_ns), BF16),
                        pltpu.VMEM((two_ns, 2 * LANES), BF16),
                        pltpu.VMEM((LANES, LANES), BF16),
                        pltpu.VMEM((rows, 2 * LANES), BF16),
                        pltpu.VMEM((rows, two_ns), F32),
                        pltpu.VMEM((rows, two_ns), BF16),
                        pltpu.VMEM((bsz, two_ns), F32),
                        pltpu.VMEM((rows + bsz, LANES), F32)],
        compiler_params=_params("parallel", "arbitrary"),
        name="ssm_core",
    )(proj, bcomp, ccomp, dcomp, a2_re, a2_im, d_row)


def _glu_out_kernel(y_ref, z_ref, wg_ref, bg_ref, wo_ref, x_ref, *rest):
    o_ref, slab_ref = rest[-2:]
    nb, tl, d = x_ref.shape
    y = jax.nn.gelu(y_ref[...])
    y = y * jax.nn.sigmoid(_mm(y.astype(BF16), wg_ref[...]) + bg_ref[...])
    y = y * jax.nn.silu(z_ref[...])
    out = _to_time_major(x_ref, slab_ref) + _mm(y.astype(BF16), wo_ref[...])
    if len(rest) == 3:
        out = _rms(out, rest[0][...])
    for c in range(d // LANES):
        slab_ref[c] = out[:, c * LANES:(c + 1) * LANES]
    for b in range(nb):
        for c in range(d // LANES):
            o_ref[b, :, c * LANES:(c + 1) * LANES] = slab_ref[c, pl.ds(b, tl, stride=nb), :]


def _glu_out(y2d, proj, w_glu, b_glu, w_out, x3d, final_gain=None):
    t, di = y2d.shape
    bsz, seqlen, d = x3d.shape
    tl = ROW_TILE // bsz
    in_specs = [pl.BlockSpec((ROW_TILE, di), lambda i: (i, 0)),
                pl.BlockSpec((ROW_TILE, di), lambda i: (i, 1)),
                _resident((di, di)),
                _resident((1, di)),
                _resident((di, d)),
                pl.BlockSpec((bsz, tl, d), lambda i: (0, i, 0))]
    args = [y2d, proj, w_glu, b_glu.reshape(1, di), w_out, x3d]
    if final_gain is not None:
        in_specs.append(_resident((1, d)))
        args.append(final_gain.reshape(1, d))
    return pl.pallas_call(
        _glu_out_kernel,
        grid=(t // ROW_TILE,),
        in_specs=in_specs,
        out_specs=pl.BlockSpec((bsz, tl, d), lambda i: (0, i, 0)),
        out_shape=jax.ShapeDtypeStruct((bsz, seqlen, d), F32),
        scratch_shapes=[pltpu.VMEM((d // LANES, ROW_TILE, LANES), F32)],
        compiler_params=_params("parallel"),
        name="glu_out",
    )(*args)


def _s5_prep(lam_re, lam_im, log_dt, b_re, b_im, c_re, c_im, d_skip):
    n, g, p = lam_re.shape
    i = b_re.shape[-1]
    gl = LANES // i
    nblk = g // gl
    ns = gl * p
    dt = jnp.exp(log_dt)[..., None]
    mag = jnp.exp(lam_re * dt)
    a_re = mag * jnp.cos(lam_im * dt)
    a_im = mag * jnp.sin(lam_im * dt)
    den = lam_re * lam_re + lam_im * lam_im
    bc_re = ((a_re - 1.0) * lam_re + a_im * lam_im) / den
    bc_im = (a_im * lam_re - (a_re - 1.0) * lam_im) / den
    bb_re = bc_re[..., None] * b_re - bc_im[..., None] * b_im
    bb_im = bc_re[..., None] * b_im + bc_im[..., None] * b_re
    ab_re = a_re[..., None] * bb_re - a_im[..., None] * bb_im
    ab_im = a_re[..., None] * bb_im + a_im[..., None] * bb_re
    ca_re = c_re * a_re[:, :, None, :] - c_im * a_im[:, :, None, :]
    ca_im = c_re * a_im[:, :, None, :] + c_im * a_re[:, :, None, :]
    cb = (jnp.einsum('ngop,ngpi->ngoi', c_re, bb_re, precision=lax.Precision.HIGHEST)
          - jnp.einsum('ngop,ngpi->ngoi', c_im, bb_im, precision=lax.Precision.HIGHEST))

    bsrc = jnp.stack([ab_re, ab_im, bb_re, bb_im], axis=1).reshape(n, 2, 2, nblk, gl, p, i)
    bcomp = jnp.transpose(bsrc, (0, 3, 1, 6, 2, 4, 5)).reshape(n, nblk, 2 * i, 2 * ns)
    csrc = jnp.stack([ca_re, -ca_im, c_re, -c_im], axis=1).reshape(n, 2, 2, nblk, gl, i, p)
    ccomp = jnp.transpose(csrc, (0, 3, 1, 5, 2, 4, 6)).reshape(n, nblk, 2 * i, 2 * ns)
    dcomp = jnp.transpose(cb.reshape(n, nblk, gl, i, i), (0, 1, 3, 2, 4)).reshape(n, nblk, i, gl * i)
    a2_re = (a_re * a_re - a_im * a_im).reshape(n, nblk, 1, ns)
    a2_im = (2.0 * a_re * a_im).reshape(n, nblk, 1, ns)
    return bcomp, ccomp, dcomp, a2_re, a2_im, d_skip.reshape(n, 1, g * i)


def _s5_layer(x2d, bsz, seqlen, norm_g, w_in, prep, layer, w_glu, b_glu, w_out, final_gain):
    x3d = x2d.reshape(bsz, seqlen, x2d.shape[1])
    proj = _norm_proj(x3d, norm_g, w_in.astype(BF16))
    y = _ssm_core(proj, prep, layer, bsz)
    out = _glu_out(y, proj, w_glu.astype(BF16), b_glu, w_out.astype(BF16), x3d, final_gain)
    return out.reshape(x2d.shape)


def kernel(x, gla_norm, gla_w_in, gla_w_gate_up, gla_b_gate, gla_head_gain, gla_w_out,
           s5_norm, s5_w_in, s5_lam_re, s5_lam_im, s5_log_dt, s5_b_re, s5_b_im,
           s5_c_re, s5_c_im, s5_d, s5_w_glu, s5_b_glu, s5_w_out, final_norm):
    bsz, seqlen, d = x.shape
    depth = gla_w_in.shape[0] + s5_w_in.shape[0]
    prep = _s5_prep(s5_lam_re, s5_lam_im, s5_log_dt, s5_b_re, s5_b_im, s5_c_re, s5_c_im, s5_d)
    x2d = x.reshape(bsz * seqlen, d)
    for i in range(depth):
        j = i // 2
        if i % 2 == 0:
            x2d = _gla_layer(x2d, bsz, seqlen, gla_norm[j], gla_w_in[j], gla_w_gate_up[j],
                             gla_b_gate[j], gla_head_gain[j], gla_w_out[j])
        else:
            x2d = _s5_layer(x2d, bsz, seqlen, s5_norm[j], s5_w_in[j], prep, j,
                            s5_w_glu[j], s5_b_glu[j], s5_w_out[j],
                            final_norm if i == depth - 1 else None)
    if depth % 2 == 1:
        x2d = _final_norm(x2d, final_norm)
    return x2d.reshape(bsz, seqlen, d)
```

```python
import functools

import jax
import jax.numpy as jnp
from jax import lax
from jax.experimental import pallas as pl
from jax.experimental.pallas import tpu as pltpu

F32 = jnp.float32
BF16 = jnp.bfloat16

EPS = 1e-6
GLA_HEADS = 4
GLA_GATE_TAU = 16.0
GLA_CHUNK = 64

LANES = 128
VMEM_LIMIT_BYTES = 56 * 1024 * 1024

ROW_TILE = 512
GLA_TILE = 256
PROJ_SLAB = 256
SSM_TILE = 1024
SSM_SUB_PAIRS = 32


def _params(*sem):
    return pltpu.CompilerParams(dimension_semantics=sem, vmem_limit_bytes=VMEM_LIMIT_BYTES)


def _mm(a, b):
    return jnp.dot(a, b, preferred_element_type=F32)


def _mm_nt(a, b):
    return lax.dot_general(a, b, (((1,), (1,)), ((), ())), preferred_element_type=F32)


def _mm_tn(a, b):
    return lax.dot_general(a, b, (((0,), (0,)), ((), ())), preferred_element_type=F32)


def _iota2(shape):
    return (lax.broadcasted_iota(jnp.int32, shape, 0), lax.broadcasted_iota(jnp.int32, shape, 1))


def _rms(x, g):
    return x * lax.rsqrt(jnp.mean(x * x, axis=-1, keepdims=True) + EPS) * g


def _resident(shape):
    return pl.BlockSpec(shape, lambda *_: (0,) * len(shape), pipeline_mode=pl.Buffered(1))


def _layer_resident(stacked, layer):
    nd = stacked.ndim
    return pl.BlockSpec((None,) + stacked.shape[1:], lambda *_: (layer,) + (0,) * (nd - 1),
                        pipeline_mode=pl.Buffered(1))


def _to_time_major(x_ref, slab_ref):
    nb, tl, d = x_ref.shape
    for b in range(nb):
        for c in range(d // LANES):
            slab_ref[c, pl.ds(b, tl, stride=nb), :] = x_ref[b, :, c * LANES:(c + 1) * LANES]
    return jnp.concatenate([slab_ref[c] for c in range(d // LANES)], axis=1)


def _norm_proj_kernel(x_ref, g_ref, w_ref, o_ref, slab_ref):
    h = _rms(_to_time_major(x_ref, slab_ref), g_ref[...])
    o_ref[...] = _mm(h.astype(BF16), w_ref[...])


def _norm_proj(x3d, gains, weights, layer):
    bsz, seqlen, d = x3d.shape
    n = weights.shape[2]
    tl = ROW_TILE // bsz
    return pl.pallas_call(
        _norm_proj_kernel,
        grid=(seqlen // tl,),
        in_specs=[pl.BlockSpec((bsz, tl, d), lambda i: (0, i, 0)),
                  _layer_resident(gains, layer),
                  _layer_resident(weights, layer)],
        out_specs=pl.BlockSpec((ROW_TILE, n), lambda i: (i, 0)),
        out_shape=jax.ShapeDtypeStruct((seqlen * bsz, n), F32),
        scratch_shapes=[pltpu.VMEM((d // LANES, ROW_TILE, LANES), F32)],
        compiler_params=_params("parallel"),
        name="norm_proj",
    )(x3d, gains, weights)


def _final_norm_kernel(x_ref, g_ref, o_ref):
    o_ref[...] = _rms(x_ref[...], g_ref[...])


def _final_norm(x2d, g):
    t, d = x2d.shape
    return pl.pallas_call(
        _final_norm_kernel,
        grid=(t // ROW_TILE,),
        in_specs=[pl.BlockSpec((ROW_TILE, d), lambda i: (i, 0)),
                  pl.BlockSpec((1, d), lambda i: (0, 0))],
        out_specs=pl.BlockSpec((ROW_TILE, d), lambda i: (i, 0)),
        out_shape=jax.ShapeDtypeStruct((t, d), F32),
        compiler_params=_params("parallel"),
        name="final_norm",
    )(x2d, g)


def _split3_bf16(x):
    hi = x.astype(BF16)
    r1 = x - hi.astype(F32)
    mid = r1.astype(BF16)
    lo = (r1 - mid.astype(F32)).astype(BF16)
    return hi, mid, lo


def _gla_project_pieces(x_ref, g_ref, win_ref, wup_ref, bg_ref, h_ref, proj_ref, bcum_ref):
    tl = x_ref.shape[0]
    c = GLA_CHUNK

    def norm():
        h_ref[...] = _rms(x_ref[...], g_ref[...]).astype(BF16)

    def slab(j):
        cols = slice(j * PROJ_SLAB, (j + 1) * PROJ_SLAB)

        def run():
            proj_ref[:, cols] = _mm(h_ref[...], win_ref[:, cols])
        return run

    def decay():
        n_main = proj_ref.shape[-1]
        r = _mm(h_ref[...], win_ref[:, n_main:n_main + LANES])
        gate = _mm(r.astype(BF16), wup_ref[...]) + bg_ref[...]
        log_a = jax.nn.log_sigmoid(gate) / GLA_GATE_TAU
        row, col = _iota2((tl, tl))
        ltri = jnp.where((row // c == col // c) & (col <= row), 1.0, 0.0).astype(BF16)
        hi, mid, lo = _split3_bf16(log_a)
        bcum_ref[...] = _mm(ltri, hi) + _mm(ltri, mid) + _mm(ltri, lo)

    return [norm] + [slab(j) for j in range(proj_ref.shape[-1] // PROJ_SLAB)] + [decay]


def _gla_mix(x_ref, gain_ref, wout_ref, o_ref, proj_ref, bcum_ref, og_ref, s_ref, first, other,
             *, heads, dk, dv):
    tl = x_ref.shape[0]
    c = GLA_CHUNK
    kd = heads * dk
    di = heads * dv
    q0, k0, v0, z0 = 0, kd, 2 * kd, 2 * kd + di

    crow, ccol = _iota2((c, c))
    causal = ccol <= crow
    drow, dcol = _iota2((dk, dk))
    eye = drow == dcol
    scale = dk ** -0.5
    gain = gain_ref[...]

    n_slots = 2 * (tl // c) * heads

    def emit_other(slot):
        for piece in other[1 + slot * (len(other) - 1) // n_slots:
                           1 + (slot + 1) * (len(other) - 1) // n_slots]:
            piece()

    other[0]()
    for ci in range(tl // c):
        sl = slice(ci * c, (ci + 1) * c)
        for hh in range(heads):
            unit = ci * heads + hh
            emit_other(2 * unit)
            bc = bcum_ref[sl, hh * dk:(hh + 1) * dk]
            bl = bc[c - 1:c, :]
            q = proj_ref[sl, q0 + hh * dk:q0 + (hh + 1) * dk]
            k = proj_ref[sl, k0 + hh * dk:k0 + (hh + 1) * dk]
            qg = (q * scale) * jnp.exp(bc)
            kg = k * jnp.exp(-bc)
            ke = k * jnp.exp(bl - bc)
            vb = proj_ref[sl, v0 + hh * dv:v0 + (hh + 1) * dv].astype(BF16)
            qgb = qg.astype(BF16)

            att = jnp.where(causal, _mm_nt(qgb, kg.astype(BF16)), 0.0)
            emit_other(2 * unit + 1)
            s_prev = s_ref[hh]
            if ci == 0:
                s_prev = jnp.where(first, 0.0, s_prev)
            o = _mm(jnp.concatenate([qgb, att.astype(BF16)], axis=1),
                    jnp.concatenate([s_prev.astype(BF16), vb], axis=0))

            dec = jnp.exp(bl)
            dec_col = jnp.sum(jnp.where(eye, jnp.broadcast_to(dec, (dk, dk)), 0.0),
                              axis=1, keepdims=True)
            s_ref[hh] = dec_col * s_prev + _mm_tn(ke.astype(BF16), vb)

            z = proj_ref[sl, z0 + hh * dv:z0 + (hh + 1) * dv]
            og_ref[sl, hh * dv:(hh + 1) * dv] = (_rms(o, gain) * jax.nn.silu(z)).astype(BF16)

    o_ref[...] = x_ref[...] + _mm(og_ref[...], wout_ref[...])


def _gla_layer_kernel(xa_ref, xb_ref, g_ref, win_ref, wup_ref, bg_ref, gain_ref, wout_ref,
                      o_ref, h_ref, proj_ref, bcum_ref, og_ref, s_ref, *, nt, heads, dk, dv):
    s = pl.program_id(0)

    @pl.when(s == 0)
    def _():
        proj_ref[1] = jnp.zeros(proj_ref.shape[1:], F32)
        bcum_ref[1] = jnp.zeros(bcum_ref.shape[1:], F32)
        s_ref[...] = jnp.zeros_like(s_ref)

    first = lax.rem(s + (nt - 1), nt) == 0
    even = lax.rem(s, 2) == 0

    def step(wr_slot, rd_slot):
        pieces = _gla_project_pieces(xa_ref, g_ref, win_ref, wup_ref, bg_ref, h_ref,
                                     proj_ref.at[wr_slot], bcum_ref.at[wr_slot])
        _gla_mix(xb_ref, gain_ref, wout_ref, o_ref, proj_ref.at[rd_slot], bcum_ref.at[rd_slot],
                 og_ref, s_ref, first, pieces, heads=heads, dk=dk, dv=dv)

    @pl.when(even)
    def _():
        step(0, 1)

    @pl.when(jnp.logical_not(even))
    def _():
        step(1, 0)


def _gla_prep(norm_g, w_in, w_gate_up, b_gate, head_gain, w_out):
    n, d, _ = w_in.shape
    _, rank, kd = w_gate_up.shape
    w_in_pad = jnp.pad(w_in, ((0, 0), (0, 0), (0, LANES - rank))).astype(BF16)
    w_up_pad = jnp.pad(w_gate_up, ((0, 0), (0, LANES - rank), (0, 0))).astype(BF16)
    return (norm_g.reshape(n, 1, d), w_in_pad, w_up_pad, b_gate.reshape(n, 1, kd),
            head_gain.reshape(n, 1, -1), w_out.astype(BF16))


def _gla_layer(x2d, bsz, seqlen, prep, layer):
    _, _, w_up_pad, _, _, w_out = prep
    t, d = x2d.shape
    kd = w_up_pad.shape[2]
    d_inner = w_out.shape[1]
    dk = kd // GLA_HEADS
    dv = d_inner // GLA_HEADS
    n_proj = 2 * kd + 2 * d_inner
    nt = seqlen // GLA_TILE
    n_tiles = bsz * nt
    return pl.pallas_call(
        functools.partial(_gla_layer_kernel, nt=nt, heads=GLA_HEADS, dk=dk, dv=dv),
        grid=(n_tiles + 1,),
        in_specs=[pl.BlockSpec((GLA_TILE, d), lambda s: (jnp.minimum(s, n_tiles - 1), 0)),
                  pl.BlockSpec((GLA_TILE, d), lambda s: (jnp.maximum(s - 1, 0), 0))]
                 + [_layer_resident(p, layer) for p in prep],
        out_specs=pl.BlockSpec((GLA_TILE, d), lambda s: (jnp.maximum(s - 1, 0), 0)),
        out_shape=jax.ShapeDtypeStruct((t, d), F32),
        scratch_shapes=[pltpu.VMEM((GLA_TILE, d), BF16),
                        pltpu.VMEM((2, GLA_TILE, n_proj), F32),
                        pltpu.VMEM((2, GLA_TILE, kd), F32),
                        pltpu.VMEM((GLA_TILE, d_inner), BF16),
                        pltpu.VMEM((GLA_HEADS, dk, dv), F32)],
        compiler_params=_params("arbitrary"),
        name="gla_layer",
    )(x2d, x2d, *prep)


def _ssm_expand(bc_ref, cc_ref, dc_ref, bm_ref, cm_ref, cb_ref):
    gi = dc_ref.shape[2]
    ns = bc_ref.shape[3] // 2
    gstates = ns // (LANES // gi)
    bc = bc_ref[0, 0].astype(BF16)
    cc = cc_ref[0, 0].astype(BF16)
    dc = dc_ref[0, 0].astype(BF16)

    r, c = _iota2((2 * LANES, 2 * gi))
    sel = jnp.where((r // LANES == c // gi) & (r % gi == c % gi), 1.0, 0.0).astype(BF16)
    r, c = _iota2((2 * LANES, 2 * ns))
    keep = (r % LANES) // gi == (c % ns) // gstates
    bm_ref[...] = jnp.where(keep, _mm(sel, bc), 0.0).astype(BF16)

    r, c = _iota2((2 * gi, 2 * LANES))
    sel_t = jnp.where((c // LANES == r // gi) & (c % gi == r % gi), 1.0, 0.0).astype(BF16)
    r, c = _iota2((2 * ns, 2 * LANES))
    keep = (r % ns) // gstates == (c % LANES) // gi
    cm_ref[...] = jnp.where(keep, _mm_tn(cc, sel_t), 0.0).astype(BF16)

    r, c = _iota2((gi, LANES))
    sel_o = jnp.where(c % gi == r, 1.0, 0.0).astype(BF16)
    r, c = _iota2((LANES, LANES))
    cb_ref[...] = jnp.where(r // gi == c // gi, _mm_tn(dc, sel_o), 0.0).astype(BF16)


def _ssm_kernel(u_ref, bc_ref, cc_ref, dc_ref, a2re_ref, a2im_ref, d_ref, y_ref,
                bm_ref, cm_ref, cb_ref, u2_ref, w_ref, tb_ref, st_ref, zsh_ref, *, nb):
    lanes = u_ref.shape[1]
    npair = u_ref.shape[0] // (2 * nb)
    rows = npair * nb
    ns = st_ref.shape[1] // 2
    sub = SSM_SUB_PAIRS
    sub_rows = sub * nb

    @pl.when(pl.program_id(1) == 0)
    def _():
        st_ref[...] = jnp.zeros_like(st_ref)
        zsh_ref[0:nb, :] = jnp.zeros((nb, lanes), F32)
        _ssm_expand(bc_ref, cc_ref, dc_ref, bm_ref, cm_ref, cb_ref)

    def even_odd(k):
        u = u_ref[2 * k * sub_rows:2 * (k + 1) * sub_rows, :].reshape(sub, 2 * nb, lanes)
        return u[:, 0:nb, :].reshape(sub_rows, lanes), u[:, nb:2 * nb, :].reshape(sub_rows, lanes)

    for k in range(npair // sub):
        ue, uo = even_odd(k)
        u2_ref[k * sub_rows:(k + 1) * sub_rows, :] = jnp.concatenate([ue, uo], axis=1).astype(BF16)

    a_re = jnp.broadcast_to(a2re_ref[0, 0], (nb, ns))
    a_im = jnp.broadcast_to(a2im_ref[0, 0], (nb, ns))
    d = d_ref[0]
    s_re = st_ref[:, 0:ns]
    s_im = st_ref[:, ns:2 * ns]

    for k in range(npair // sub):
        blk = slice(k * sub_rows, (k + 1) * sub_rows)
        w_ref[blk, :] = _mm(u2_ref[blk, :], bm_ref[...])
        for m in range(0, sub, 2):
            r0 = k * sub_rows + m * nb
            m_re = a_re * s_re - a_im * s_im + w_ref[r0:r0 + nb, 0:ns]
            m_im = a_re * s_im + a_im * s_re + w_ref[r0:r0 + nb, ns:2 * ns]
            s_re = a_re * m_re - a_im * m_im + w_ref[r0 + nb:r0 + 2 * nb, 0:ns]
            s_im = a_re * m_im + a_im * m_re + w_ref[r0 + nb:r0 + 2 * nb, ns:2 * ns]
            tb_ref[r0:r0 + 2 * nb, 0:ns] = jnp.concatenate([m_re, s_re], axis=0).astype(BF16)
            tb_ref[r0:r0 + 2 * nb, ns:2 * ns] = jnp.concatenate([m_im, s_im], axis=0).astype(BF16)
        z = _mm(tb_ref[blk, :], cm_ref[...])
        zsh_ref[nb + k * sub_rows:nb + (k + 1) * sub_rows, :] = z[:, 0:lanes]
        ue, uo = even_odd(k)
        ye = zsh_ref[blk, :] + _mm(u2_ref[blk, 0:lanes], cb_ref[...]) + d * ue
        yo = z[:, lanes:2 * lanes] + d * uo
        y_ref[2 * k * sub_rows:2 * (k + 1) * sub_rows, :] = jnp.concatenate(
            [ye.reshape(sub, nb, lanes), yo.reshape(sub, nb, lanes)], axis=1).reshape(2 * sub_rows, lanes)

    st_ref[:, 0:ns] = s_re
    st_ref[:, ns:2 * ns] = s_im
    zsh_ref[0:nb, :] = zsh_ref[rows:rows + nb, :]


def _ssm_core(proj, prep, layer, bsz):
    bcomp, ccomp, dcomp, a2_re, a2_im, d_row = prep
    _, nblk, gi2, two_ns = bcomp.shape
    d_inner = nblk * LANES
    tl = SSM_TILE
    rows = (tl // 2) * bsz
    per_block = lambda *blk: pl.BlockSpec((1, 1) + blk, lambda j, i: (layer, j, 0, 0))
    return pl.pallas_call(
        functools.partial(_ssm_kernel, nb=bsz),
        grid=(nblk, proj.shape[0] // (tl * bsz)),
        in_specs=[
            pl.BlockSpec((tl * bsz, LANES), lambda j, i: (i, j)),
            per_block(gi2, two_ns),
            per_block(gi2, two_ns),
            per_block(gi2 // 2, LANES),
            per_block(1, two_ns // 2),
            per_block(1, two_ns // 2),
            pl.BlockSpec((1, 1, LANES), lambda j, i: (layer, 0, j)),
        ],
        out_specs=pl.BlockSpec((tl * bsz, LANES), lambda j, i: (i, j)),
        out_shape=jax.ShapeDtypeStruct((proj.shape[0], d_inner), F32),
        scratch_shapes=[pltpu.VMEM((2 * LANES, two_ns), BF16),
                        pltpu.VMEM((two_ns, 2 * LANES), BF16),
                        pltpu.VMEM((LANES, LANES), BF16),
                        pltpu.VMEM((rows, 2 * LANES), BF16),
                        pltpu.VMEM((rows, two_ns), F32),
                        pltpu.VMEM((rows, two_ns), BF16),
                        pltpu.VMEM((bsz, two_ns), F32),
                        pltpu.VMEM((rows + bsz, LANES), F32)],
        compiler_params=_params("parallel", "arbitrary"),
        name="ssm_core",
    )(proj, bcomp, ccomp, dcomp, a2_re, a2_im, d_row)


def _glu_out_kernel(y_ref, z_ref, wg_ref, bg_ref, wo_ref, x_ref, *rest):
    o_ref, slab_ref = rest[-2:]
    nb, tl, d = x_ref.shape
    y = jax.nn.gelu(y_ref[...])
    y = y * jax.nn.sigmoid(_mm(y.astype(BF16), wg_ref[...]) + bg_ref[...])
    y = y * jax.nn.silu(z_ref[...])
    out = _to_time_major(x_ref, slab_ref) + _mm(y.astype(BF16), wo_ref[...])
    if len(rest) == 3:
        out = _rms(out, rest[0][...])
    for c in range(d // LANES):
        slab_ref[c] = out[:, c * LANES:(c + 1) * LANES]
    for b in range(nb):
        for c in range(d // LANES):
            o_ref[b, :, c * LANES:(c + 1) * LANES] = slab_ref[c, pl.ds(b, tl, stride=nb), :]


def _glu_out(y2d, proj, w_glu, b_glu, w_out, layer, x3d, final_gain=None):
    t, di = y2d.shape
    bsz, seqlen, d = x3d.shape
    tl = ROW_TILE // bsz
    in_specs = [pl.BlockSpec((ROW_TILE, di), lambda i: (i, 0)),
                pl.BlockSpec((ROW_TILE, di), lambda i: (i, 1)),
                _layer_resident(w_glu, layer),
                _layer_resident(b_glu, layer),
                _layer_resident(w_out, layer),
                pl.BlockSpec((bsz, tl, d), lambda i: (0, i, 0))]
    args = [y2d, proj, w_glu, b_glu, w_out, x3d]
    if final_gain is not None:
        in_specs.append(_resident((1, d)))
        args.append(final_gain)
    return pl.pallas_call(
        _glu_out_kernel,
        grid=(t // ROW_TILE,),
        in_specs=in_specs,
        out_specs=pl.BlockSpec((bsz, tl, d), lambda i: (0, i, 0)),
        out_shape=jax.ShapeDtypeStruct((bsz, seqlen, d), F32),
        scratch_shapes=[pltpu.VMEM((d // LANES, ROW_TILE, LANES), F32)],
        compiler_params=_params("parallel"),
        name="glu_out",
    )(*args)


def _s5_prep(lam_re, lam_im, log_dt, b_re, b_im, c_re, c_im, d_skip):
    n, g, p = lam_re.shape
    i = b_re.shape[-1]
    gl = LANES // i
    nblk = g // gl
    ns = gl * p
    dt = jnp.exp(log_dt)[..., None]
    mag = jnp.exp(lam_re * dt)
    a_re = mag * jnp.cos(lam_im * dt)
    a_im = mag * jnp.sin(lam_im * dt)
    den = lam_re * lam_re + lam_im * lam_im
    bc_re = ((a_re - 1.0) * lam_re + a_im * lam_im) / den
    bc_im = (a_im * lam_re - (a_re - 1.0) * lam_im) / den
    bb_re = bc_re[..., None] * b_re - bc_im[..., None] * b_im
    bb_im = bc_re[..., None] * b_im + bc_im[..., None] * b_re
    ab_re = a_re[..., None] * bb_re - a_im[..., None] * bb_im
    ab_im = a_re[..., None] * bb_im + a_im[..., None] * bb_re
    ca_re = c_re * a_re[:, :, None, :] - c_im * a_im[:, :, None, :]
    ca_im = c_re * a_im[:, :, None, :] + c_im * a_re[:, :, None, :]
    cb = (jnp.einsum('ngop,ngpi->ngoi', c_re, bb_re, precision=lax.Precision.HIGHEST)
          - jnp.einsum('ngop,ngpi->ngoi', c_im, bb_im, precision=lax.Precision.HIGHEST))

    bsrc = jnp.stack([ab_re, ab_im, bb_re, bb_im], axis=1).reshape(n, 2, 2, nblk, gl, p, i)
    bcomp = jnp.transpose(bsrc, (0, 3, 1, 6, 2, 4, 5)).reshape(n, nblk, 2 * i, 2 * ns)
    csrc = jnp.stack([ca_re, -ca_im, c_re, -c_im], axis=1).reshape(n, 2, 2, nblk, gl, i, p)
    ccomp = jnp.transpose(csrc, (0, 3, 1, 5, 2, 4, 6)).reshape(n, nblk, 2 * i, 2 * ns)
    dcomp = jnp.transpose(cb.reshape(n, nblk, gl, i, i), (0, 1, 3, 2, 4)).reshape(n, nblk, i, gl * i)
    a2_re = (a_re * a_re - a_im * a_im).reshape(n, nblk, 1, ns)
    a2_im = (2.0 * a_re * a_im).reshape(n, nblk, 1, ns)
    return bcomp, ccomp, dcomp, a2_re, a2_im, d_skip.reshape(n, 1, g * i)


def _s5_layer(x2d, bsz, seqlen, dense, prep, layer, final_gain):
    norm_g, w_in, w_glu, b_glu, w_out = dense
    x3d = x2d.reshape(bsz, seqlen, x2d.shape[1])
    proj = _norm_proj(x3d, norm_g, w_in, layer)
    y = _ssm_core(proj, prep, layer, bsz)
    out = _glu_out(y, proj, w_glu, b_glu, w_out, layer, x3d, final_gain)
    return out.reshape(x2d.shape)


def kernel(x, gla_norm, gla_w_in, gla_w_gate_up, gla_b_gate, gla_head_gain, gla_w_out,
           s5_norm, s5_w_in, s5_lam_re, s5_lam_im, s5_log_dt, s5_b_re, s5_b_im,
           s5_c_re, s5_c_im, s5_d, s5_w_glu, s5_b_glu, s5_w_out, final_norm):
    bsz, seqlen, d = x.shape
    n_s5 = s5_w_in.shape[0]
    depth = gla_w_in.shape[0] + n_s5
    gla_prep = _gla_prep(gla_norm, gla_w_in, gla_w_gate_up, gla_b_gate, gla_head_gain, gla_w_out)
    s5_dense = (s5_norm.reshape(n_s5, 1, d), s5_w_in.astype(BF16), s5_w_glu.astype(BF16),
                s5_b_glu.reshape(n_s5, 1, -1), s5_w_out.astype(BF16))
    s5_prep = _s5_prep(s5_lam_re, s5_lam_im, s5_log_dt, s5_b_re, s5_b_im, s5_c_re, s5_c_im, s5_d)
    final_gain = final_norm.reshape(1, d)
    x2d = x.reshape(bsz * seqlen, d)
    for i in range(depth):
        j = i // 2
        if i % 2 == 0:
            x2d = _gla_layer(x2d, bsz, seqlen, gla_prep, j)
        else:
            x2d = _s5_layer(x2d, bsz, seqlen, s5_dense, s5_prep, j,
                            final_gain if i == depth - 1 else None)
    if depth % 2 == 1:
        x2d = _final_norm(x2d, final_gain)
    return x2d.reshape(bsz, seqlen, d)
```

```python
import functools

import jax
import jax.numpy as jnp
from jax import lax
from jax.experimental import pallas as pl
from jax.experimental.pallas import tpu as pltpu

F32 = jnp.float32
BF16 = jnp.bfloat16

EPS = 1e-6
GLA_HEADS = 4
GLA_GATE_TAU = 16.0
GLA_CHUNK = 64

LANES = 128
VMEM_LIMIT_BYTES = 56 * 1024 * 1024

ROW_TILE = 512
GLA_TILE = 256
PROJ_SLAB = 256
SSM_TILE = 1024
SSM_SUB_PAIRS = 32


def _params(*sem):
    return pltpu.CompilerParams(dimension_semantics=sem, vmem_limit_bytes=VMEM_LIMIT_BYTES)


def _mm(a, b):
    return jnp.dot(a, b, preferred_element_type=F32)


def _mm_nt(a, b):
    return lax.dot_general(a, b, (((1,), (1,)), ((), ())), preferred_element_type=F32)


def _mm_tn(a, b):
    return lax.dot_general(a, b, (((0,), (0,)), ((), ())), preferred_element_type=F32)


def _iota2(shape):
    return (lax.broadcasted_iota(jnp.int32, shape, 0), lax.broadcasted_iota(jnp.int32, shape, 1))


def _rms(x, g):
    return x * lax.rsqrt(jnp.mean(x * x, axis=-1, keepdims=True) + EPS) * g


def _resident(shape):
    return pl.BlockSpec(shape, lambda *_: (0,) * len(shape), pipeline_mode=pl.Buffered(1))


def _layer_resident(stacked, layer):
    nd = stacked.ndim
    return pl.BlockSpec((None,) + stacked.shape[1:], lambda *_: (layer,) + (0,) * (nd - 1),
                        pipeline_mode=pl.Buffered(1))


def _to_time_major(x_ref, slab_ref):
    nb, tl, d = x_ref.shape
    for b in range(nb):
        for c in range(d // LANES):
            slab_ref[c, pl.ds(b, tl, stride=nb), :] = x_ref[b, :, c * LANES:(c + 1) * LANES]
    return jnp.concatenate([slab_ref[c] for c in range(d // LANES)], axis=1)


def _norm_proj_kernel(x_ref, g_ref, w_ref, o_ref, slab_ref):
    h = _rms(_to_time_major(x_ref, slab_ref), g_ref[...])
    o_ref[...] = _mm(h.astype(BF16), w_ref[...])


def _norm_proj(x3d, gains, weights, layer):
    bsz, seqlen, d = x3d.shape
    n = weights.shape[2]
    tl = ROW_TILE // bsz
    return pl.pallas_call(
        _norm_proj_kernel,
        grid=(seqlen // tl,),
        in_specs=[pl.BlockSpec((bsz, tl, d), lambda i: (0, i, 0)),
                  _layer_resident(gains, layer),
                  _layer_resident(weights, layer)],
        out_specs=pl.BlockSpec((ROW_TILE, n), lambda i: (i, 0)),
        out_shape=jax.ShapeDtypeStruct((seqlen * bsz, n), F32),
        scratch_shapes=[pltpu.VMEM((d // LANES, ROW_TILE, LANES), F32)],
        compiler_params=_params("parallel"),
        name="norm_proj",
    )(x3d, gains, weights)


def _final_norm_kernel(x_ref, g_ref, o_ref):
    o_ref[...] = _rms(x_ref[...], g_ref[...])


def _final_norm(x2d, g):
    t, d = x2d.shape
    return pl.pallas_call(
        _final_norm_kernel,
        grid=(t // ROW_TILE,),
        in_specs=[pl.BlockSpec((ROW_TILE, d), lambda i: (i, 0)),
                  pl.BlockSpec((1, d), lambda i: (0, 0))],
        out_specs=pl.BlockSpec((ROW_TILE, d), lambda i: (i, 0)),
        out_shape=jax.ShapeDtypeStruct((t, d), F32),
        compiler_params=_params("parallel"),
        name="final_norm",
    )(x2d, g)


def _split3_bf16(x):
    hi = x.astype(BF16)
    r1 = x - hi.astype(F32)
    mid = r1.astype(BF16)
    lo = (r1 - mid.astype(F32)).astype(BF16)
    return hi, mid, lo


def _gla_project_pieces(x_ref, g_ref, win_ref, wup_ref, bg_ref, h_ref, r_ref, la_ref, proj_ref, bcum_ref):
    tl = x_ref.shape[0]
    c = GLA_CHUNK

    def norm():
        h_ref[...] = _rms(x_ref[...], g_ref[...]).astype(BF16)

    def slab(j):
        cols = slice(j * PROJ_SLAB, (j + 1) * PROJ_SLAB)

        def run():
            proj_ref[:, cols] = _mm(h_ref[...], win_ref[:, cols])
        return run

    def gate_in():
        n_main = proj_ref.shape[-1]
        r_ref[...] = _mm(h_ref[...], win_ref[:, n_main:n_main + LANES]).astype(BF16)

    def log_decay():
        log_a = jax.nn.log_sigmoid(_mm(r_ref[...], wup_ref[...]) + bg_ref[...]) / GLA_GATE_TAU
        for i, part in enumerate(_split3_bf16(log_a)):
            la_ref[i] = part

    def cumsum():
        row, col = _iota2((tl, tl))
        ltri = jnp.where((row // c == col // c) & (col <= row), 1.0, 0.0).astype(BF16)
        bcum_ref[...] = _mm(ltri, la_ref[0]) + _mm(ltri, la_ref[1]) + _mm(ltri, la_ref[2])

    slabs = [slab(j) for j in range(proj_ref.shape[-1] // PROJ_SLAB)]
    third = len(slabs) // 3
    return [norm, gate_in] + slabs[:third] + [log_decay] + slabs[third:2 * third] + [cumsum] + slabs[2 * third:]


def _gla_mix(x_ref, gain_ref, wout_ref, o_ref, proj_ref, bcum_ref, og_ref, s_ref, first, other,
             *, heads, dk, dv):
    tl = x_ref.shape[0]
    c = GLA_CHUNK
    kd = heads * dk
    di = heads * dv
    q0, k0, v0, z0 = 0, kd, 2 * kd, 2 * kd + di

    crow, ccol = _iota2((c, c))
    causal = ccol <= crow
    drow, dcol = _iota2((dk, dk))
    eye = drow == dcol
    scale = dk ** -0.5
    gain = gain_ref[...]

    n_slots = 2 * (tl // c) * heads

    def emit_other(slot):
        for piece in other[1 + slot * (len(other) - 1) // n_slots:
                           1 + (slot + 1) * (len(other) - 1) // n_slots]:
            piece()

    def scores(unit):
        ci, hh = divmod(unit, heads)
        sl = slice(ci * c, (ci + 1) * c)
        bc = bcum_ref[sl, hh * dk:(hh + 1) * dk]
        bl = bc[c - 1:c, :]
        q = proj_ref[sl, q0 + hh * dk:q0 + (hh + 1) * dk]
        k = proj_ref[sl, k0 + hh * dk:k0 + (hh + 1) * dk]
        qgb = ((q * scale) * jnp.exp(bc)).astype(BF16)
        kg = k * jnp.exp(-bc)
        keb = (k * jnp.exp(bl - bc)).astype(BF16)
        att = jnp.where(causal, _mm_nt(qgb, kg.astype(BF16)), 0.0).astype(BF16)
        return qgb, keb, att, bl

    def outputs(unit, qgb, keb, att, bl):
        ci, hh = divmod(unit, heads)
        sl = slice(ci * c, (ci + 1) * c)
        vb = proj_ref[sl, v0 + hh * dv:v0 + (hh + 1) * dv].astype(BF16)
        s_prev = s_ref[hh]
        if ci == 0:
            s_prev = jnp.where(first, 0.0, s_prev)
        o = _mm(jnp.concatenate([qgb, att], axis=1),
                jnp.concatenate([s_prev.astype(BF16), vb], axis=0))

        dec = jnp.exp(bl)
        dec_col = jnp.sum(jnp.where(eye, jnp.broadcast_to(dec, (dk, dk)), 0.0),
                          axis=1, keepdims=True)
        s_ref[hh] = dec_col * s_prev + _mm_tn(keb, vb)

        z = proj_ref[sl, z0 + hh * dv:z0 + (hh + 1) * dv]
        og_ref[sl, hh * dv:(hh + 1) * dv] = (_rms(o, gain) * jax.nn.silu(z)).astype(BF16)

    n_units = (tl // c) * heads
    other[0]()
    ready = scores(0)
    for unit in range(n_units):
        emit_other(2 * unit)
        nxt = scores(unit + 1) if unit + 1 < n_units else None
        emit_other(2 * unit + 1)
        outputs(unit, *ready)
        ready = nxt

    o_ref[...] = x_ref[...] + _mm(og_ref[...], wout_ref[...])


def _gla_layer_kernel(xa_ref, xb_ref, g_ref, win_ref, wup_ref, bg_ref, gain_ref, wout_ref,
                      o_ref, h_ref, r_ref, la_ref, proj_ref, bcum_ref, og_ref, s_ref, *, nt, heads, dk, dv):
    s = pl.program_id(0)

    @pl.when(s == 0)
    def _():
        proj_ref[1] = jnp.zeros(proj_ref.shape[1:], F32)
        bcum_ref[1] = jnp.zeros(bcum_ref.shape[1:], F32)
        s_ref[...] = jnp.zeros_like(s_ref)

    first = lax.rem(s + (nt - 1), nt) == 0
    even = lax.rem(s, 2) == 0

    def step(wr_slot, rd_slot):
        pieces = _gla_project_pieces(xa_ref, g_ref, win_ref, wup_ref, bg_ref, h_ref, r_ref, la_ref,
                                     proj_ref.at[wr_slot], bcum_ref.at[wr_slot])
        _gla_mix(xb_ref, gain_ref, wout_ref, o_ref, proj_ref.at[rd_slot], bcum_ref.at[rd_slot],
                 og_ref, s_ref, first, pieces, heads=heads, dk=dk, dv=dv)

    @pl.when(even)
    def _():
        step(0, 1)

    @pl.when(jnp.logical_not(even))
    def _():
        step(1, 0)


def _gla_prep(norm_g, w_in, w_gate_up, b_gate, head_gain, w_out):
    n, d, _ = w_in.shape
    _, rank, kd = w_gate_up.shape
    w_in_pad = jnp.pad(w_in, ((0, 0), (0, 0), (0, LANES - rank))).astype(BF16)
    w_up_pad = jnp.pad(w_gate_up, ((0, 0), (0, LANES - rank), (0, 0))).astype(BF16)
    return (norm_g.reshape(n, 1, d), w_in_pad, w_up_pad, b_gate.reshape(n, 1, kd),
            head_gain.reshape(n, 1, -1), w_out.astype(BF16))


def _gla_layer(x2d, bsz, seqlen, prep, layer):
    _, _, w_up_pad, _, _, w_out = prep
    t, d = x2d.shape
    kd = w_up_pad.shape[2]
    d_inner = w_out.shape[1]
    dk = kd // GLA_HEADS
    dv = d_inner // GLA_HEADS
    n_proj = 2 * kd + 2 * d_inner
    nt = seqlen // GLA_TILE
    n_tiles = bsz * nt
    return pl.pallas_call(
        functools.partial(_gla_layer_kernel, nt=nt, heads=GLA_HEADS, dk=dk, dv=dv),
        grid=(n_tiles + 1,),
        in_specs=[pl.BlockSpec((GLA_TILE, d), lambda s: (jnp.minimum(s, n_tiles - 1), 0)),
                  pl.BlockSpec((GLA_TILE, d), lambda s: (jnp.maximum(s - 1, 0), 0))]
                 + [_layer_resident(p, layer) for p in prep],
        out_specs=pl.BlockSpec((GLA_TILE, d), lambda s: (jnp.maximum(s - 1, 0), 0)),
        out_shape=jax.ShapeDtypeStruct((t, d), F32),
        scratch_shapes=[pltpu.VMEM((GLA_TILE, d), BF16),
                        pltpu.VMEM((GLA_TILE, LANES), BF16),
                        pltpu.VMEM((3, GLA_TILE, kd), BF16),
                        pltpu.VMEM((2, GLA_TILE, n_proj), F32),
                        pltpu.VMEM((2, GLA_TILE, kd), F32),
                        pltpu.VMEM((GLA_TILE, d_inner), BF16),
                        pltpu.VMEM((GLA_HEADS, dk, dv), F32)],
        compiler_params=_params("arbitrary"),
        name="gla_layer",
    )(x2d, x2d, *prep)


def _ssm_expand(bc_ref, cc_ref, dc_ref, bm_ref, cm_ref, cb_ref):
    gi = dc_ref.shape[2]
    ns = bc_ref.shape[3] // 2
    gstates = ns // (LANES // gi)
    bc = bc_ref[0, 0].astype(BF16)
    cc = cc_ref[0, 0].astype(BF16)
    dc = dc_ref[0, 0].astype(BF16)

    r, c = _iota2((2 * LANES, 2 * gi))
    sel = jnp.where((r // LANES == c // gi) & (r % gi == c % gi), 1.0, 0.0).astype(BF16)
    r, c = _iota2((2 * LANES, 2 * ns))
    keep = (r % LANES) // gi == (c % ns) // gstates
    bm_ref[...] = jnp.where(keep, _mm(sel, bc), 0.0).astype(BF16)

    r, c = _iota2((2 * gi, 2 * LANES))
    sel_t = jnp.where((c // LANES == r // gi) & (c % gi == r % gi), 1.0, 0.0).astype(BF16)
    r, c = _iota2((2 * ns, 2 * LANES))
    keep = (r % ns) // gstates == (c % LANES) // gi
    cm_ref[...] = jnp.where(keep, _mm_tn(cc, sel_t), 0.0).astype(BF16)

    r, c = _iota2((gi, LANES))
    sel_o = jnp.where(c % gi == r, 1.0, 0.0).astype(BF16)
    r, c = _iota2((LANES, LANES))
    cb_ref[...] = jnp.where(r // gi == c // gi, _mm_tn(dc, sel_o), 0.0).astype(BF16)


def _ssm_kernel(u_ref, bc_ref, cc_ref, dc_ref, a2re_ref, a2im_ref, d_ref, y_ref,
                bm_ref, cm_ref, cb_ref, u2_ref, w_ref, tb_ref, st_ref, zsh_ref, *, nb):
    lanes = u_ref.shape[1]
    npair = u_ref.shape[0] // (2 * nb)
    rows = npair * nb
    ns = st_ref.shape[1] // 2
    sub = SSM_SUB_PAIRS
    sub_rows = sub * nb

    @pl.when(pl.program_id(1) == 0)
    def _():
        st_ref[...] = jnp.zeros_like(st_ref)
        zsh_ref[0:nb, :] = jnp.zeros((nb, lanes), F32)
        _ssm_expand(bc_ref, cc_ref, dc_ref, bm_ref, cm_ref, cb_ref)

    def even_odd(k):
        u = u_ref[2 * k * sub_rows:2 * (k + 1) * sub_rows, :].reshape(sub, 2 * nb, lanes)
        return u[:, 0:nb, :].reshape(sub_rows, lanes), u[:, nb:2 * nb, :].reshape(sub_rows, lanes)

    for k in range(npair // sub):
        ue, uo = even_odd(k)
        u2_ref[k * sub_rows:(k + 1) * sub_rows, :] = jnp.concatenate([ue, uo], axis=1).astype(BF16)

    a_re = jnp.broadcast_to(a2re_ref[0, 0], (nb, ns))
    a_im = jnp.broadcast_to(a2im_ref[0, 0], (nb, ns))
    d = d_ref[0]
    s_re = st_ref[:, 0:ns]
    s_im = st_ref[:, ns:2 * ns]

    for k in range(npair // sub):
        blk = slice(k * sub_rows, (k + 1) * sub_rows)
        w_ref[blk, :] = _mm(u2_ref[blk, :], bm_ref[...])
        for m in range(0, sub, 2):
            r0 = k * sub_rows + m * nb
            m_re = a_re * s_re - a_im * s_im + w_ref[r0:r0 + nb, 0:ns]
            m_im = a_re * s_im + a_im * s_re + w_ref[r0:r0 + nb, ns:2 * ns]
            s_re = a_re * m_re - a_im * m_im + w_ref[r0 + nb:r0 + 2 * nb, 0:ns]
            s_im = a_re * m_im + a_im * m_re + w_ref[r0 + nb:r0 + 2 * nb, ns:2 * ns]
            tb_ref[r0:r0 + 2 * nb, 0:ns] = jnp.concatenate([m_re, s_re], axis=0).astype(BF16)
            tb_ref[r0:r0 + 2 * nb, ns:2 * ns] = jnp.concatenate([m_im, s_im], axis=0).astype(BF16)
        z = _mm(tb_ref[blk, :], cm_ref[...])
        zsh_ref[nb + k * sub_rows:nb + (k + 1) * sub_rows, :] = z[:, 0:lanes]
        ue, uo = even_odd(k)
        ye = zsh_ref[blk, :] + _mm(u2_ref[blk, 0:lanes], cb_ref[...]) + d * ue
        yo = z[:, lanes:2 * lanes] + d * uo
        y_ref[2 * k * sub_rows:2 * (k + 1) * sub_rows, :] = jnp.concatenate(
            [ye.reshape(sub, nb, lanes), yo.reshape(sub, nb, lanes)], axis=1).reshape(2 * sub_rows, lanes)

    st_ref[:, 0:ns] = s_re
    st_ref[:, ns:2 * ns] = s_im
    zsh_ref[0:nb, :] = zsh_ref[rows:rows + nb, :]


def _ssm_core(proj, prep, layer, bsz):
    bcomp, ccomp, dcomp, a2_re, a2_im, d_row = prep
    _, nblk, gi2, two_ns = bcomp.shape
    d_inner = nblk * LANES
    tl = SSM_TILE
    rows = (tl // 2) * bsz
    per_block = lambda *blk: pl.BlockSpec((1, 1) + blk, lambda j, i: (layer, j, 0, 0))
    return pl.pallas_call(
        functools.partial(_ssm_kernel, nb=bsz),
        grid=(nblk, proj.shape[0] // (tl * bsz)),
        in_specs=[
            pl.BlockSpec((tl * bsz, LANES), lambda j, i: (i, j)),
            per_block(gi2, two_ns),
            per_block(gi2, two_ns),
            per_block(gi2 // 2, LANES),
            per_block(1, two_ns // 2),
            per_block(1, two_ns // 2),
            pl.BlockSpec((1, 1, LANES), lambda j, i: (layer, 0, j)),
        ],
        out_specs=pl.BlockSpec((tl * bsz, LANES), lambda j, i: (i, j)),
        out_shape=jax.ShapeDtypeStruct((proj.shape[0], d_inner), F32),
        scratch_shapes=[pltpu.VMEM((2 * LANES, two_ns), BF16),
                        pltpu.VMEM((two_ns, 2 * LANES), BF16),
                        pltpu.VMEM((LANES, LANES), BF16),
                        pltpu.VMEM((rows, 2 * LANES), BF16),
                        pltpu.VMEM((rows, two_ns), F32),
                        pltpu.VMEM((rows, two_ns), BF16),
                        pltpu.VMEM((bsz, two_ns), F32),
                        pltpu.VMEM((rows + bsz, LANES), F32)],
        compiler_params=_params("parallel", "arbitrary"),
        name="ssm_core",
    )(proj, bcomp, ccomp, dcomp, a2_re, a2_im, d_row)


def _glu_out_kernel(y_ref, z_ref, wg_ref, bg_ref, wo_ref, x_ref, *rest):
    o_ref, slab_ref = rest[-2:]
    nb, tl, d = x_ref.shape
    y = jax.nn.gelu(y_ref[...])
    y = y * jax.nn.sigmoid(_mm(y.astype(BF16), wg_ref[...]) + bg_ref[...])
    y = y * jax.nn.silu(z_ref[...])
    out = _to_time_major(x_ref, slab_ref) + _mm(y.astype(BF16), wo_ref[...])
    if len(rest) == 3:
        out = _rms(out, rest[0][...])
    for c in range(d // LANES):
        slab_ref[c] = out[:, c * LANES:(c + 1) * LANES]
    for b in range(nb):
        for c in range(d // LANES):
            o_ref[b, :, c * LANES:(c + 1) * LANES] = slab_ref[c, pl.ds(b, tl, stride=nb), :]


def _glu_out(y2d, proj, w_glu, b_glu, w_out, layer, x3d, final_gain=None):
    t, di = y2d.shape
    bsz, seqlen, d = x3d.shape
    tl = ROW_TILE // bsz
    in_specs = [pl.BlockSpec((ROW_TILE, di), lambda i: (i, 0)),
                pl.BlockSpec((ROW_TILE, di), lambda i: (i, 1)),
                _layer_resident(w_glu, layer),
                _layer_resident(b_glu, layer),
                _layer_resident(w_out, layer),
                pl.BlockSpec((bsz, tl, d), lambda i: (0, i, 0))]
    args = [y2d, proj, w_glu, b_glu, w_out, x3d]
    if final_gain is not None:
        in_specs.append(_resident((1, d)))
        args.append(final_gain)
    return pl.pallas_call(
        _glu_out_kernel,
        grid=(t // ROW_TILE,),
        in_specs=in_specs,
        out_specs=pl.BlockSpec((bsz, tl, d), lambda i: (0, i, 0)),
        out_shape=jax.ShapeDtypeStruct((bsz, seqlen, d), F32),
        scratch_shapes=[pltpu.VMEM((d // LANES, ROW_TILE, LANES), F32)],
        compiler_params=_params("parallel"),
        name="glu_out",
    )(*args)


def _s5_prep(lam_re, lam_im, log_dt, b_re, b_im, c_re, c_im, d_skip):
    n, g, p = lam_re.shape
    i = b_re.shape[-1]
    gl = LANES // i
    nblk = g // gl
    ns = gl * p
    dt = jnp.exp(log_dt)[..., None]
    mag = jnp.exp(lam_re * dt)
    a_re = mag * jnp.cos(lam_im * dt)
    a_im = mag * jnp.sin(lam_im * dt)
    den = lam_re * lam_re + lam_im * lam_im
    bc_re = ((a_re - 1.0) * lam_re + a_im * lam_im) / den
    bc_im = (a_im * lam_re - (a_re - 1.0) * lam_im) / den
    bb_re = bc_re[..., None] * b_re - bc_im[..., None] * b_im
    bb_im = bc_re[..., None] * b_im + bc_im[..., None] * b_re
    ab_re = a_re[..., None] * bb_re - a_im[..., None] * bb_im
    ab_im = a_re[..., None] * bb_im + a_im[..., None] * bb_re
    ca_re = c_re * a_re[:, :, None, :] - c_im * a_im[:, :, None, :]
    ca_im = c_re * a_im[:, :, None, :] + c_im * a_re[:, :, None, :]
    cb = (jnp.einsum('ngop,ngpi->ngoi', c_re, bb_re, precision=lax.Precision.HIGHEST)
          - jnp.einsum('ngop,ngpi->ngoi', c_im, bb_im, precision=lax.Precision.HIGHEST))

    bsrc = jnp.stack([ab_re, ab_im, bb_re, bb_im], axis=1).reshape(n, 2, 2, nblk, gl, p, i)
    bcomp = jnp.transpose(bsrc, (0, 3, 1, 6, 2, 4, 5)).reshape(n, nblk, 2 * i, 2 * ns)
    csrc = jnp.stack([ca_re, -ca_im, c_re, -c_im], axis=1).reshape(n, 2, 2, nblk, gl, i, p)
    ccomp = jnp.transpose(csrc, (0, 3, 1, 5, 2, 4, 6)).reshape(n, nblk, 2 * i, 2 * ns)
    dcomp = jnp.transpose(cb.reshape(n, nblk, gl, i, i), (0, 1, 3, 2, 4)).reshape(n, nblk, i, gl * i)
    a2_re = (a_re * a_re - a_im * a_im).reshape(n, nblk, 1, ns)
    a2_im = (2.0 * a_re * a_im).reshape(n, nblk, 1, ns)
    return bcomp, ccomp, dcomp, a2_re, a2_im, d_skip.reshape(n, 1, g * i)


def _s5_layer(x2d, bsz, seqlen, dense, prep, layer, final_gain):
    norm_g, w_in, w_glu, b_glu, w_out = dense
    x3d = x2d.reshape(bsz, seqlen, x2d.shape[1])
    proj = _norm_proj(x3d, norm_g, w_in, layer)
    y = _ssm_core(proj, prep, layer, bsz)
    out = _glu_out(y, proj, w_glu, b_glu, w_out, layer, x3d, final_gain)
    return out.reshape(x2d.shape)


def kernel(x, gla_norm, gla_w_in, gla_w_gate_up, gla_b_gate, gla_head_gain, gla_w_out,
           s5_norm, s5_w_in, s5_lam_re, s5_lam_im, s5_log_dt, s5_b_re, s5_b_im,
           s5_c_re, s5_c_im, s5_d, s5_w_glu, s5_b_glu, s5_w_out, final_norm):
    bsz, seqlen, d = x.shape
    n_s5 = s5_w_in.shape[0]
    depth = gla_w_in.shape[0] + n_s5
    gla_prep = _gla_prep(gla_norm, gla_w_in, gla_w_gate_up, gla_b_gate, gla_head_gain, gla_w_out)
    s5_dense = (s5_norm.reshape(n_s5, 1, d), s5_w_in.astype(BF16), s5_w_glu.astype(BF16),
                s5_b_glu.reshape(n_s5, 1, -1), s5_w_out.astype(BF16))
    s5_prep = _s5_prep(s5_lam_re, s5_lam_im, s5_log_dt, s5_b_re, s5_b_im, s5_c_re, s5_c_im, s5_d)
    final_gain = final_norm.reshape(1, d)
    x2d = x.reshape(bsz * seqlen, d)
    for i in range(depth):
        j = i // 2
        if i % 2 == 0:
            x2d = _gla_layer(x2d, bsz, seqlen, gla_prep, j)
        else:
            x2d = _s5_layer(x2d, bsz, seqlen, s5_dense, s5_prep, j,
                            final_gain if i == depth - 1 else None)
    if depth % 2 == 1:
        x2d = _final_norm(x2d, final_gain)
    return x2d.reshape(bsz, seqlen, d)
```

```python
import functools

import jax
import jax.numpy as jnp
from jax import lax
from jax.experimental import pallas as pl
from jax.experimental.pallas import tpu as pltpu

F32 = jnp.float32
BF16 = jnp.bfloat16

EPS = 1e-6
GLA_HEADS = 4
GLA_GATE_TAU = 16.0
GLA_CHUNK = 64

LANES = 128
VMEM_LIMIT_BYTES = 56 * 1024 * 1024

ROW_TILE = 512
GLA_TILE = 256
PROJ_SLAB = 256
SSM_TILE = 1024
SSM_SUB_PAIRS = 64


def _params(*sem):
    return pltpu.CompilerParams(dimension_semantics=sem, vmem_limit_bytes=VMEM_LIMIT_BYTES)


def _mm(a, b):
    return jnp.dot(a, b, preferred_element_type=F32)


def _mm_nt(a, b):
    return lax.dot_general(a, b, (((1,), (1,)), ((), ())), preferred_element_type=F32)


def _mm_tn(a, b):
    return lax.dot_general(a, b, (((0,), (0,)), ((), ())), preferred_element_type=F32)


def _iota2(shape):
    return (lax.broadcasted_iota(jnp.int32, shape, 0), lax.broadcasted_iota(jnp.int32, shape, 1))


def _rms(x, g):
    return x * lax.rsqrt(jnp.mean(x * x, axis=-1, keepdims=True) + EPS) * g


def _resident(shape):
    return pl.BlockSpec(shape, lambda *_: (0,) * len(shape), pipeline_mode=pl.Buffered(1))


def _layer_resident(stacked, layer):
    nd = stacked.ndim
    return pl.BlockSpec((None,) + stacked.shape[1:], lambda *_: (layer,) + (0,) * (nd - 1),
                        pipeline_mode=pl.Buffered(1))


def _to_time_major(x_ref, slab_ref):
    nb, tl, d = x_ref.shape
    for b in range(nb):
        for c in range(d // LANES):
            slab_ref[c, pl.ds(b, tl, stride=nb), :] = x_ref[b, :, c * LANES:(c + 1) * LANES]
    return jnp.concatenate([slab_ref[c] for c in range(d // LANES)], axis=1)


def _norm_proj_kernel(x_ref, g_ref, w_ref, o_ref, slab_ref):
    h = _rms(_to_time_major(x_ref, slab_ref), g_ref[...])
    o_ref[...] = _mm(h.astype(BF16), w_ref[...])


def _norm_proj(x3d, gains, weights, layer):
    bsz, seqlen, d = x3d.shape
    n = weights.shape[2]
    tl = ROW_TILE // bsz
    return pl.pallas_call(
        _norm_proj_kernel,
        grid=(seqlen // tl,),
        in_specs=[pl.BlockSpec((bsz, tl, d), lambda i: (0, i, 0)),
                  _layer_resident(gains, layer),
                  _layer_resident(weights, layer)],
        out_specs=pl.BlockSpec((ROW_TILE, n), lambda i: (i, 0)),
        out_shape=jax.ShapeDtypeStruct((seqlen * bsz, n), F32),
        scratch_shapes=[pltpu.VMEM((d // LANES, ROW_TILE, LANES), F32)],
        compiler_params=_params("parallel"),
        name="norm_proj",
    )(x3d, gains, weights)


def _final_norm_kernel(x_ref, g_ref, o_ref):
    o_ref[...] = _rms(x_ref[...], g_ref[...])


def _final_norm(x2d, g):
    t, d = x2d.shape
    return pl.pallas_call(
        _final_norm_kernel,
        grid=(t // ROW_TILE,),
        in_specs=[pl.BlockSpec((ROW_TILE, d), lambda i: (i, 0)),
                  pl.BlockSpec((1, d), lambda i: (0, 0))],
        out_specs=pl.BlockSpec((ROW_TILE, d), lambda i: (i, 0)),
        out_shape=jax.ShapeDtypeStruct((t, d), F32),
        compiler_params=_params("parallel"),
        name="final_norm",
    )(x2d, g)


def _split3_bf16(x):
    hi = x.astype(BF16)
    r1 = x - hi.astype(F32)
    mid = r1.astype(BF16)
    lo = (r1 - mid.astype(F32)).astype(BF16)
    return hi, mid, lo


def _gla_project_pieces(x_ref, g_ref, win_ref, wup_ref, bg_ref, h_ref, r_ref, la_ref, proj_ref, bcum_ref):
    tl = x_ref.shape[0]
    c = GLA_CHUNK

    def norm():
        h_ref[...] = _rms(x_ref[...], g_ref[...]).astype(BF16)

    def slab(j):
        cols = slice(j * PROJ_SLAB, (j + 1) * PROJ_SLAB)

        def run():
            proj_ref[:, cols] = _mm(h_ref[...], win_ref[:, cols])
        return run

    def gate_in():
        n_main = proj_ref.shape[-1]
        r_ref[...] = _mm(h_ref[...], win_ref[:, n_main:n_main + LANES]).astype(BF16)

    def log_decay():
        log_a = jax.nn.log_sigmoid(_mm(r_ref[...], wup_ref[...]) + bg_ref[...]) / GLA_GATE_TAU
        for i, part in enumerate(_split3_bf16(log_a)):
            la_ref[i] = part

    def cumsum():
        row, col = _iota2((tl, tl))
        ltri = jnp.where((row // c == col // c) & (col <= row), 1.0, 0.0).astype(BF16)
        bcum_ref[...] = _mm(ltri, la_ref[0]) + _mm(ltri, la_ref[1]) + _mm(ltri, la_ref[2])

    slabs = [slab(j) for j in range(proj_ref.shape[-1] // PROJ_SLAB)]
    third = len(slabs) // 3
    return [norm, gate_in] + slabs[:third] + [log_decay] + slabs[third:2 * third] + [cumsum] + slabs[2 * third:]


def _gla_mix(x_ref, gain_ref, wout_ref, o_ref, proj_ref, bcum_ref, og_ref, s_ref, first, other,
             *, heads, dk, dv):
    tl = x_ref.shape[0]
    c = GLA_CHUNK
    kd = heads * dk
    di = heads * dv
    q0, k0, v0, z0 = 0, kd, 2 * kd, 2 * kd + di

    crow, ccol = _iota2((c, c))
    causal = ccol <= crow
    drow, dcol = _iota2((dk, dk))
    eye = drow == dcol
    scale = dk ** -0.5
    gain = gain_ref[...]

    n_slots = 2 * (tl // c) * heads

    def emit_other(slot):
        for piece in other[1 + slot * (len(other) - 1) // n_slots:
                           1 + (slot + 1) * (len(other) - 1) // n_slots]:
            piece()

    def scores(unit):
        ci, hh = divmod(unit, heads)
        sl = slice(ci * c, (ci + 1) * c)
        bc = bcum_ref[sl, hh * dk:(hh + 1) * dk]
        bl = bc[c - 1:c, :]
        q = proj_ref[sl, q0 + hh * dk:q0 + (hh + 1) * dk]
        k = proj_ref[sl, k0 + hh * dk:k0 + (hh + 1) * dk]
        qgb = ((q * scale) * jnp.exp(bc)).astype(BF16)
        kg = k * jnp.exp(-bc)
        keb = (k * jnp.exp(bl - bc)).astype(BF16)
        att = jnp.where(causal, _mm_nt(qgb, kg.astype(BF16)), 0.0).astype(BF16)
        return qgb, keb, att, bl

    def outputs(unit, qgb, keb, att, bl):
        ci, hh = divmod(unit, heads)
        sl = slice(ci * c, (ci + 1) * c)
        vb = proj_ref[sl, v0 + hh * dv:v0 + (hh + 1) * dv].astype(BF16)
        s_prev = s_ref[hh]
        if ci == 0:
            s_prev = jnp.where(first, 0.0, s_prev)
        o = _mm(jnp.concatenate([qgb, att], axis=1),
                jnp.concatenate([s_prev.astype(BF16), vb], axis=0))

        dec = jnp.exp(bl)
        dec_col = jnp.sum(jnp.where(eye, jnp.broadcast_to(dec, (dk, dk)), 0.0),
                          axis=1, keepdims=True)
        s_ref[hh] = dec_col * s_prev + _mm_tn(keb, vb)

        z = proj_ref[sl, z0 + hh * dv:z0 + (hh + 1) * dv]
        og_ref[sl, hh * dv:(hh + 1) * dv] = (_rms(o, gain) * jax.nn.silu(z)).astype(BF16)

    n_units = (tl // c) * heads
    other[0]()
    ready = scores(0)
    for unit in range(n_units):
        emit_other(2 * unit)
        nxt = scores(unit + 1) if unit + 1 < n_units else None
        emit_other(2 * unit + 1)
        outputs(unit, *ready)
        ready = nxt

    o_ref[...] = x_ref[...] + _mm(og_ref[...], wout_ref[...])


def _gla_layer_kernel(xa_ref, xb_ref, g_ref, win_ref, wup_ref, bg_ref, gain_ref, wout_ref,
                      o_ref, h_ref, r_ref, la_ref, proj_ref, bcum_ref, og_ref, s_ref, *, nt, heads, dk, dv):
    s = pl.program_id(0)

    @pl.when(s == 0)
    def _():
        proj_ref[1] = jnp.zeros(proj_ref.shape[1:], F32)
        bcum_ref[1] = jnp.zeros(bcum_ref.shape[1:], F32)
        s_ref[...] = jnp.zeros_like(s_ref)

    first = lax.rem(s + (nt - 1), nt) == 0
    even = lax.rem(s, 2) == 0

    def step(wr_slot, rd_slot):
        pieces = _gla_project_pieces(xa_ref, g_ref, win_ref, wup_ref, bg_ref, h_ref, r_ref, la_ref,
                                     proj_ref.at[wr_slot], bcum_ref.at[wr_slot])
        _gla_mix(xb_ref, gain_ref, wout_ref, o_ref, proj_ref.at[rd_slot], bcum_ref.at[rd_slot],
                 og_ref, s_ref, first, pieces, heads=heads, dk=dk, dv=dv)

    @pl.when(even)
    def _():
        step(0, 1)

    @pl.when(jnp.logical_not(even))
    def _():
        step(1, 0)


def _gla_prep(norm_g, w_in, w_gate_up, b_gate, head_gain, w_out):
    n, d, _ = w_in.shape
    _, rank, kd = w_gate_up.shape
    w_in_pad = jnp.pad(w_in, ((0, 0), (0, 0), (0, LANES - rank))).astype(BF16)
    w_up_pad = jnp.pad(w_gate_up, ((0, 0), (0, LANES - rank), (0, 0))).astype(BF16)
    return (norm_g.reshape(n, 1, d), w_in_pad, w_up_pad, b_gate.reshape(n, 1, kd),
            head_gain.reshape(n, 1, -1), w_out.astype(BF16))


def _gla_layer(x2d, bsz, seqlen, prep, layer):
    _, _, w_up_pad, _, _, w_out = prep
    t, d = x2d.shape
    kd = w_up_pad.shape[2]
    d_inner = w_out.shape[1]
    dk = kd // GLA_HEADS
    dv = d_inner // GLA_HEADS
    n_proj = 2 * kd + 2 * d_inner
    nt = seqlen // GLA_TILE
    n_tiles = bsz * nt
    return pl.pallas_call(
        functools.partial(_gla_layer_kernel, nt=nt, heads=GLA_HEADS, dk=dk, dv=dv),
        grid=(n_tiles + 1,),
        in_specs=[pl.BlockSpec((GLA_TILE, d), lambda s: (jnp.minimum(s, n_tiles - 1), 0)),
                  pl.BlockSpec((GLA_TILE, d), lambda s: (jnp.maximum(s - 1, 0), 0))]
                 + [_layer_resident(p, layer) for p in prep],
        out_specs=pl.BlockSpec((GLA_TILE, d), lambda s: (jnp.maximum(s - 1, 0), 0)),
        out_shape=jax.ShapeDtypeStruct((t, d), F32),
        scratch_shapes=[pltpu.VMEM((GLA_TILE, d), BF16),
                        pltpu.VMEM((GLA_TILE, LANES), BF16),
                        pltpu.VMEM((3, GLA_TILE, kd), BF16),
                        pltpu.VMEM((2, GLA_TILE, n_proj), F32),
                        pltpu.VMEM((2, GLA_TILE, kd), F32),
                        pltpu.VMEM((GLA_TILE, d_inner), BF16),
                        pltpu.VMEM((GLA_HEADS, dk, dv), F32)],
        compiler_params=_params("arbitrary"),
        name="gla_layer",
    )(x2d, x2d, *prep)


def _ssm_expand(bc_ref, cc_ref, dc_ref, bm_ref, cm_ref, cb_ref):
    gi = dc_ref.shape[2]
    ns = bc_ref.shape[3] // 2
    gstates = ns // (LANES // gi)
    bc = bc_ref[0, 0].astype(BF16)
    cc = cc_ref[0, 0].astype(BF16)
    dc = dc_ref[0, 0].astype(BF16)

    r, c = _iota2((2 * LANES, 2 * gi))
    sel = jnp.where((r // LANES == c // gi) & (r % gi == c % gi), 1.0, 0.0).astype(BF16)
    r, c = _iota2((2 * LANES, 2 * ns))
    keep = (r % LANES) // gi == (c % ns) // gstates
    bm_ref[...] = jnp.where(keep, _mm(sel, bc), 0.0).astype(BF16)

    r, c = _iota2((2 * gi, 2 * LANES))
    sel_t = jnp.where((c // LANES == r // gi) & (c % gi == r % gi), 1.0, 0.0).astype(BF16)
    r, c = _iota2((2 * ns, 2 * LANES))
    keep = (r % ns) // gstates == (c % LANES) // gi
    cm_ref[...] = jnp.where(keep, _mm_tn(cc, sel_t), 0.0).astype(BF16)

    r, c = _iota2((gi, LANES))
    sel_o = jnp.where(c % gi == r, 1.0, 0.0).astype(BF16)
    r, c = _iota2((LANES, LANES))
    cb_ref[...] = jnp.where(r // gi == c // gi, _mm_tn(dc, sel_o), 0.0).astype(BF16)


def _ssm_kernel(u_ref, bc_ref, cc_ref, dc_ref, a2re_ref, a2im_ref, d_ref, y_ref,
                bm_ref, cm_ref, cb_ref, u2_ref, w_ref, tb_ref, st_ref, zsh_ref, *, nb):
    lanes = u_ref.shape[1]
    npair = u_ref.shape[0] // (2 * nb)
    rows = npair * nb
    ns = st_ref.shape[1] // 2
    sub = SSM_SUB_PAIRS
    sub_rows = sub * nb

    @pl.when(pl.program_id(1) == 0)
    def _():
        st_ref[...] = jnp.zeros_like(st_ref)
        zsh_ref[0:nb, :] = jnp.zeros((nb, lanes), F32)
        _ssm_expand(bc_ref, cc_ref, dc_ref, bm_ref, cm_ref, cb_ref)

    def even_odd(k):
        u = u_ref[2 * k * sub_rows:2 * (k + 1) * sub_rows, :].reshape(sub, 2 * nb, lanes)
        return u[:, 0:nb, :].reshape(sub_rows, lanes), u[:, nb:2 * nb, :].reshape(sub_rows, lanes)

    for k in range(npair // sub):
        ue, uo = even_odd(k)
        u2_ref[k * sub_rows:(k + 1) * sub_rows, :] = jnp.concatenate([ue, uo], axis=1).astype(BF16)

    a_re = jnp.broadcast_to(a2re_ref[0, 0], (nb, ns))
    a_im = jnp.broadcast_to(a2im_ref[0, 0], (nb, ns))
    d = d_ref[0]
    s_re = st_ref[:, 0:ns]
    s_im = st_ref[:, ns:2 * ns]

    def drive(k):
        blk = slice(k * sub_rows, (k + 1) * sub_rows)
        w_ref[blk, :] = _mm(u2_ref[blk, :], bm_ref[...])

    def readout(k):
        blk = slice(k * sub_rows, (k + 1) * sub_rows)
        z = _mm(tb_ref[blk, :], cm_ref[...])
        zsh_ref[nb + k * sub_rows:nb + (k + 1) * sub_rows, :] = z[:, 0:lanes]
        ue, uo = even_odd(k)
        ye = zsh_ref[blk, :] + _mm(u2_ref[blk, 0:lanes], cb_ref[...]) + d * ue
        yo = z[:, lanes:2 * lanes] + d * uo
        y_ref[2 * k * sub_rows:2 * (k + 1) * sub_rows, :] = jnp.concatenate(
            [ye.reshape(sub, nb, lanes), yo.reshape(sub, nb, lanes)], axis=1).reshape(2 * sub_rows, lanes)

    n_sub = npair // sub
    drive(0)
    for k in range(n_sub):
        if k + 1 < n_sub:
            drive(k + 1)
        if k >= 1:
            readout(k - 1)
        for m in range(0, sub, 2):
            r0 = k * sub_rows + m * nb
            m_re = a_re * s_re - a_im * s_im + w_ref[r0:r0 + nb, 0:ns]
            m_im = a_re * s_im + a_im * s_re + w_ref[r0:r0 + nb, ns:2 * ns]
            s_re = a_re * m_re - a_im * m_im + w_ref[r0 + nb:r0 + 2 * nb, 0:ns]
            s_im = a_re * m_im + a_im * m_re + w_ref[r0 + nb:r0 + 2 * nb, ns:2 * ns]
            tb_ref[r0:r0 + 2 * nb, 0:ns] = jnp.concatenate([m_re, s_re], axis=0).astype(BF16)
            tb_ref[r0:r0 + 2 * nb, ns:2 * ns] = jnp.concatenate([m_im, s_im], axis=0).astype(BF16)
    readout(n_sub - 1)

    st_ref[:, 0:ns] = s_re
    st_ref[:, ns:2 * ns] = s_im
    zsh_ref[0:nb, :] = zsh_ref[rows:rows + nb, :]


def _ssm_core(proj, prep, layer, bsz):
    bcomp, ccomp, dcomp, a2_re, a2_im, d_row = prep
    _, nblk, gi2, two_ns = bcomp.shape
    d_inner = nblk * LANES
    tl = SSM_TILE
    rows = (tl // 2) * bsz
    per_block = lambda *blk: pl.BlockSpec((1, 1) + blk, lambda j, i: (layer, j, 0, 0))
    return pl.pallas_call(
        functools.partial(_ssm_kernel, nb=bsz),
        grid=(nblk, proj.shape[0] // (tl * bsz)),
        in_specs=[
            pl.BlockSpec((tl * bsz, LANES), lambda j, i: (i, j)),
            per_block(gi2, two_ns),
            per_block(gi2, two_ns),
            per_block(gi2 // 2, LANES),
            per_block(1, two_ns // 2),
            per_block(1, two_ns // 2),
            pl.BlockSpec((1, 1, LANES), lambda j, i: (layer, 0, j)),
        ],
        out_specs=pl.BlockSpec((tl * bsz, LANES), lambda j, i: (i, j)),
        out_shape=jax.ShapeDtypeStruct((proj.shape[0], d_inner), F32),
        scratch_shapes=[pltpu.VMEM((2 * LANES, two_ns), BF16),
                        pltpu.VMEM((two_ns, 2 * LANES), BF16),
                        pltpu.VMEM((LANES, LANES), BF16),
                        pltpu.VMEM((rows, 2 * LANES), BF16),
                        pltpu.VMEM((rows, two_ns), F32),
                        pltpu.VMEM((rows, two_ns), BF16),
                        pltpu.VMEM((bsz, two_ns), F32),
                        pltpu.VMEM((rows + bsz, LANES), F32)],
        compiler_params=_params("parallel", "arbitrary"),
        name="ssm_core",
    )(proj, bcomp, ccomp, dcomp, a2_re, a2_im, d_row)


def _glu_out_kernel(y_ref, z_ref, wg_ref, bg_ref, wo_ref, x_ref, *rest):
    o_ref, slab_ref = rest[-2:]
    nb, tl, d = x_ref.shape
    y = jax.nn.gelu(y_ref[...])
    y = y * jax.nn.sigmoid(_mm(y.astype(BF16), wg_ref[...]) + bg_ref[...])
    y = y * jax.nn.silu(z_ref[...])
    out = _to_time_major(x_ref, slab_ref) + _mm(y.astype(BF16), wo_ref[...])
    if len(rest) == 3:
        out = _rms(out, rest[0][...])
    for c in range(d // LANES):
        slab_ref[c] = out[:, c * LANES:(c + 1) * LANES]
    for b in range(nb):
        for c in range(d // LANES):
            o_ref[b, :, c * LANES:(c + 1) * LANES] = slab_ref[c, pl.ds(b, tl, stride=nb), :]


def _glu_out(y2d, proj, w_glu, b_glu, w_out, layer, x3d, final_gain=None):
    t, di = y2d.shape
    bsz, seqlen, d = x3d.shape
    tl = ROW_TILE // bsz
    in_specs = [pl.BlockSpec((ROW_TILE, di), lambda i: (i, 0)),
                pl.BlockSpec((ROW_TILE, di), lambda i: (i, 1)),
                _layer_resident(w_glu, layer),
                _layer_resident(b_glu, layer),
                _layer_resident(w_out, layer),
                pl.BlockSpec((bsz, tl, d), lambda i: (0, i, 0))]
    args = [y2d, proj, w_glu, b_glu, w_out, x3d]
    if final_gain is not None:
        in_specs.append(_resident((1, d)))
        args.append(final_gain)
    return pl.pallas_call(
        _glu_out_kernel,
        grid=(t // ROW_TILE,),
        in_specs=in_specs,
        out_specs=pl.BlockSpec((bsz, tl, d), lambda i: (0, i, 0)),
        out_shape=jax.ShapeDtypeStruct((bsz, seqlen, d), F32),
        scratch_shapes=[pltpu.VMEM((d // LANES, ROW_TILE, LANES), F32)],
        compiler_params=_params("parallel"),
        name="glu_out",
    )(*args)


def _s5_prep(lam_re, lam_im, log_dt, b_re, b_im, c_re, c_im, d_skip):
    n, g, p = lam_re.shape
    i = b_re.shape[-1]
    gl = LANES // i
    nblk = g // gl
    ns = gl * p
    dt = jnp.exp(log_dt)[..., None]
    mag = jnp.exp(lam_re * dt)
    a_re = mag * jnp.cos(lam_im * dt)
    a_im = mag * jnp.sin(lam_im * dt)
    den = lam_re * lam_re + lam_im * lam_im
    bc_re = ((a_re - 1.0) * lam_re + a_im * lam_im) / den
    bc_im = (a_im * lam_re - (a_re - 1.0) * lam_im) / den
    bb_re = bc_re[..., None] * b_re - bc_im[..., None] * b_im
    bb_im = bc_re[..., None] * b_im + bc_im[..., None] * b_re
    ab_re = a_re[..., None] * bb_re - a_im[..., None] * bb_im
    ab_im = a_re[..., None] * bb_im + a_im[..., None] * bb_re
    ca_re = c_re * a_re[:, :, None, :] - c_im * a_im[:, :, None, :]
    ca_im = c_re * a_im[:, :, None, :] + c_im * a_re[:, :, None, :]
    cb = (jnp.einsum('ngop,ngpi->ngoi', c_re, bb_re, precision=lax.Precision.HIGHEST)
          - jnp.einsum('ngop,ngpi->ngoi', c_im, bb_im, precision=lax.Precision.HIGHEST))

    bsrc = jnp.stack([ab_re, ab_im, bb_re, bb_im], axis=1).reshape(n, 2, 2, nblk, gl, p, i)
    bcomp = jnp.transpose(bsrc, (0, 3, 1, 6, 2, 4, 5)).reshape(n, nblk, 2 * i, 2 * ns)
    csrc = jnp.stack([ca_re, -ca_im, c_re, -c_im], axis=1).reshape(n, 2, 2, nblk, gl, i, p)
    ccomp = jnp.transpose(csrc, (0, 3, 1, 5, 2, 4, 6)).reshape(n, nblk, 2 * i, 2 * ns)
    dcomp = jnp.transpose(cb.reshape(n, nblk, gl, i, i), (0, 1, 3, 2, 4)).reshape(n, nblk, i, gl * i)
    a2_re = (a_re * a_re - a_im * a_im).reshape(n, nblk, 1, ns)
    a2_im = (2.0 * a_re * a_im).reshape(n, nblk, 1, ns)
    return bcomp, ccomp, dcomp, a2_re, a2_im, d_skip.reshape(n, 1, g * i)


def _s5_layer(x2d, bsz, seqlen, dense, prep, layer, final_gain):
    norm_g, w_in, w_glu, b_glu, w_out = dense
    x3d = x2d.reshape(bsz, seqlen, x2d.shape[1])
    proj = _norm_proj(x3d, norm_g, w_in, layer)
    y = _ssm_core(proj, prep, layer, bsz)
    out = _glu_out(y, proj, w_glu, b_glu, w_out, layer, x3d, final_gain)
    return out.reshape(x2d.shape)


def kernel(x, gla_norm, gla_w_in, gla_w_gate_up, gla_b_gate, gla_head_gain, gla_w_out,
           s5_norm, s5_w_in, s5_lam_re, s5_lam_im, s5_log_dt, s5_b_re, s5_b_im,
           s5_c_re, s5_c_im, s5_d, s5_w_glu, s5_b_glu, s5_w_out, final_norm):
    bsz, seqlen, d = x.shape
    n_s5 = s5_w_in.shape[0]
    depth = gla_w_in.shape[0] + n_s5
    gla_prep = _gla_prep(gla_norm, gla_w_in, gla_w_gate_up, gla_b_gate, gla_head_gain, gla_w_out)
    s5_dense = (s5_norm.reshape(n_s5, 1, d), s5_w_in.astype(BF16), s5_w_glu.astype(BF16),
                s5_b_glu.reshape(n_s5, 1, -1), s5_w_out.astype(BF16))
    s5_prep = _s5_prep(s5_lam_re, s5_lam_im, s5_log_dt, s5_b_re, s5_b_im, s5_c_re, s5_c_im, s5_d)
    final_gain = final_norm.reshape(1, d)
    x2d = x.reshape(bsz * seqlen, d)
    for i in range(depth):
        j = i // 2
        if i % 2 == 0:
            x2d = _gla_layer(x2d, bsz, seqlen, gla_prep, j)
        else:
            x2d = _s5_layer(x2d, bsz, seqlen, s5_dense, s5_prep, j,
                            final_gain if i == depth - 1 else None)
    if depth % 2 == 1:
        x2d = _final_norm(x2d, final_gain)
    return x2d.reshape(bsz, seqlen, d)
```

```python
import functools

import jax
import jax.numpy as jnp
from jax import lax
from jax.experimental import pallas as pl
from jax.experimental.pallas import tpu as pltpu

F32 = jnp.float32
BF16 = jnp.bfloat16

EPS = 1e-6
GLA_HEADS = 4
GLA_GATE_TAU = 16.0
GLA_CHUNK = 64

LANES = 128
VMEM_LIMIT_BYTES = 56 * 1024 * 1024

ROW_TILE = 512
GLA_TILE = 256
PROJ_SLAB = 256
SSM_TILE = 1024
SSM_SUB_PAIRS = 64


def _params(*sem):
    return pltpu.CompilerParams(dimension_semantics=sem, vmem_limit_bytes=VMEM_LIMIT_BYTES)


def _mm(a, b):
    return jnp.dot(a, b, preferred_element_type=F32)


def _mm_nt(a, b):
    return lax.dot_general(a, b, (((1,), (1,)), ((), ())), preferred_element_type=F32)


def _mm_tn(a, b):
    return lax.dot_general(a, b, (((0,), (0,)), ((), ())), preferred_element_type=F32)


def _iota2(shape):
    return (lax.broadcasted_iota(jnp.int32, shape, 0), lax.broadcasted_iota(jnp.int32, shape, 1))


def _rms(x, g):
    return x * lax.rsqrt(jnp.mean(x * x, axis=-1, keepdims=True) + EPS) * g


def _resident(shape):
    return pl.BlockSpec(shape, lambda *_: (0,) * len(shape), pipeline_mode=pl.Buffered(1))


def _layer_resident(stacked, layer):
    nd = stacked.ndim
    return pl.BlockSpec((None,) + stacked.shape[1:], lambda *_: (layer,) + (0,) * (nd - 1),
                        pipeline_mode=pl.Buffered(1))


def _to_time_major(x_ref, slab_ref):
    nb, tl, d = x_ref.shape
    for b in range(nb):
        for c in range(d // LANES):
            slab_ref[c, pl.ds(b, tl, stride=nb), :] = x_ref[b, :, c * LANES:(c + 1) * LANES]
    return jnp.concatenate([slab_ref[c] for c in range(d // LANES)], axis=1)


def _norm_proj_kernel(x_ref, g_ref, w_ref, o_ref, slab_ref):
    h = _rms(_to_time_major(x_ref, slab_ref), g_ref[...])
    o_ref[...] = _mm(h.astype(BF16), w_ref[...])


def _norm_proj(x3d, gains, weights, layer):
    bsz, seqlen, d = x3d.shape
    n = weights.shape[2]
    tl = ROW_TILE // bsz
    return pl.pallas_call(
        _norm_proj_kernel,
        grid=(seqlen // tl,),
        in_specs=[pl.BlockSpec((bsz, tl, d), lambda i: (0, i, 0)),
                  _layer_resident(gains, layer),
                  _layer_resident(weights, layer)],
        out_specs=pl.BlockSpec((ROW_TILE, n), lambda i: (i, 0)),
        out_shape=jax.ShapeDtypeStruct((seqlen * bsz, n), F32),
        scratch_shapes=[pltpu.VMEM((d // LANES, ROW_TILE, LANES), F32)],
        compiler_params=_params("parallel"),
        name="norm_proj",
    )(x3d, gains, weights)


def _final_norm_kernel(x_ref, g_ref, o_ref):
    o_ref[...] = _rms(x_ref[...], g_ref[...])


def _final_norm(x2d, g):
    t, d = x2d.shape
    return pl.pallas_call(
        _final_norm_kernel,
        grid=(t // ROW_TILE,),
        in_specs=[pl.BlockSpec((ROW_TILE, d), lambda i: (i, 0)),
                  pl.BlockSpec((1, d), lambda i: (0, 0))],
        out_specs=pl.BlockSpec((ROW_TILE, d), lambda i: (i, 0)),
        out_shape=jax.ShapeDtypeStruct((t, d), F32),
        compiler_params=_params("parallel"),
        name="final_norm",
    )(x2d, g)


def _split3_bf16(x):
    hi = x.astype(BF16)
    r1 = x - hi.astype(F32)
    mid = r1.astype(BF16)
    lo = (r1 - mid.astype(F32)).astype(BF16)
    return hi, mid, lo


def _gla_project_pieces(x_ref, g_ref, win_ref, wup_ref, bg_ref, h_ref, r_ref, la_ref, proj_ref, bcum_ref):
    tl = x_ref.shape[0]
    c = GLA_CHUNK

    def norm():
        h_ref[...] = _rms(x_ref[...], g_ref[...]).astype(BF16)

    def slab(j):
        cols = slice(j * PROJ_SLAB, (j + 1) * PROJ_SLAB)

        def run():
            proj_ref[:, cols] = _mm(h_ref[...], win_ref[:, cols])
        return run

    def gate_in():
        n_main = proj_ref.shape[-1]
        r_ref[...] = _mm(h_ref[...], win_ref[:, n_main:n_main + LANES]).astype(BF16)

    def log_decay():
        log_a = jax.nn.log_sigmoid(_mm(r_ref[...], wup_ref[...]) + bg_ref[...]) / GLA_GATE_TAU
        for i, part in enumerate(_split3_bf16(log_a)):
            la_ref[i] = part

    def cumsum():
        row, col = _iota2((tl, tl))
        ltri = jnp.where((row // c == col // c) & (col <= row), 1.0, 0.0).astype(BF16)
        bcum_ref[...] = _mm(ltri, la_ref[0]) + _mm(ltri, la_ref[1]) + _mm(ltri, la_ref[2])

    slabs = [slab(j) for j in range(proj_ref.shape[-1] // PROJ_SLAB)]
    third = len(slabs) // 3
    return [norm, gate_in] + slabs[:third] + [log_decay] + slabs[third:2 * third] + [cumsum] + slabs[2 * third:]


def _gla_mix(x_ref, gain_ref, wout_ref, o_ref, proj_ref, bcum_ref, og_ref, s_ref, first, other,
             *, heads, dk, dv):
    tl = x_ref.shape[0]
    c = GLA_CHUNK
    kd = heads * dk
    di = heads * dv
    q0, k0, v0, z0 = 0, kd, 2 * kd, 2 * kd + di

    c2 = 2 * c
    prow, pcol = _iota2((c, c2))
    top_keep = pcol <= prow
    bot_keep = (pcol >= c) & (pcol - c <= prow)
    first_rows = lax.broadcasted_iota(jnp.int32, (c2, dk), 0) < c
    drow, dcol = _iota2((dk, dk))
    eye = drow == dcol
    scale = dk ** -0.5
    gain = gain_ref[...]

    n_units = (tl // c2) * heads
    n_slots = 2 * n_units

    def emit_other(slot):
        for piece in other[1 + slot * (len(other) - 1) // n_slots:
                           1 + (slot + 1) * (len(other) - 1) // n_slots]:
            piece()

    def scores(unit):
        cp, hh = divmod(unit, heads)
        sl = slice(cp * c2, (cp + 1) * c2)
        bc = bcum_ref[sl, hh * dk:(hh + 1) * dk]
        bl1 = bc[c - 1:c, :]
        bl2 = bc[c2 - 1:c2, :]
        dec1 = jnp.exp(bl1)
        dec2 = jnp.exp(bl2)
        q = proj_ref[sl, q0 + hh * dk:q0 + (hh + 1) * dk]
        k = proj_ref[sl, k0 + hh * dk:k0 + (hh + 1) * dk]
        qg = (q * scale) * jnp.exp(bc)
        qgb = qg.astype(BF16)
        kgb = (k * jnp.exp(-bc)).astype(BF16)
        ke = k * jnp.exp(jnp.where(first_rows, bl1, bl2) - bc)
        ke1z = jnp.where(first_rows, ke, 0.0).astype(BF16)
        s_all = _mm_nt(qgb, kgb)
        att = jnp.concatenate(
            [jnp.where(top_keep, s_all[0:c], 0.0),
             jnp.where(bot_keep, s_all[c:c2], 0.0) + _mm_nt(qgb[c:c2], ke1z)], axis=0).astype(BF16)
        lhs_q = jnp.where(first_rows, qg, qg * dec1).astype(BF16)
        kpb = jnp.where(first_rows, ke * dec2, ke).astype(BF16)
        return lhs_q, kpb, att, bl1 + bl2

    def outputs(unit, lhs_q, kpb, att, bl12):
        cp, hh = divmod(unit, heads)
        sl = slice(cp * c2, (cp + 1) * c2)
        vb = proj_ref[sl, v0 + hh * dv:v0 + (hh + 1) * dv].astype(BF16)
        s_prev = s_ref[hh]
        if cp == 0:
            s_prev = jnp.where(first, 0.0, s_prev)
        o = _mm(jnp.concatenate([lhs_q, att], axis=1),
                jnp.concatenate([s_prev.astype(BF16), vb], axis=0))

        dec = jnp.exp(bl12)
        dec_col = jnp.sum(jnp.where(eye, jnp.broadcast_to(dec, (dk, dk)), 0.0),
                          axis=1, keepdims=True)
        s_ref[hh] = dec_col * s_prev + _mm_tn(kpb, vb)

        z = proj_ref[sl, z0 + hh * dv:z0 + (hh + 1) * dv]
        og_ref[sl, hh * dv:(hh + 1) * dv] = (_rms(o, gain) * jax.nn.silu(z)).astype(BF16)

    other[0]()
    ready = scores(0)
    for unit in range(n_units):
        emit_other(2 * unit)
        nxt = scores(unit + 1) if unit + 1 < n_units else None
        emit_other(2 * unit + 1)
        outputs(unit, *ready)
        ready = nxt

    o_ref[...] = x_ref[...] + _mm(og_ref[...], wout_ref[...])


def _gla_layer_kernel(xa_ref, xb_ref, g_ref, win_ref, wup_ref, bg_ref, gain_ref, wout_ref,
                      o_ref, h_ref, r_ref, la_ref, proj_ref, bcum_ref, og_ref, s_ref, *, nt, heads, dk, dv):
    s = pl.program_id(0)

    @pl.when(s == 0)
    def _():
        proj_ref[1] = jnp.zeros(proj_ref.shape[1:], F32)
        bcum_ref[1] = jnp.zeros(bcum_ref.shape[1:], F32)
        s_ref[...] = jnp.zeros_like(s_ref)

    first = lax.rem(s + (nt - 1), nt) == 0
    even = lax.rem(s, 2) == 0

    def step(wr_slot, rd_slot):
        pieces = _gla_project_pieces(xa_ref, g_ref, win_ref, wup_ref, bg_ref, h_ref, r_ref, la_ref,
                                     proj_ref.at[wr_slot], bcum_ref.at[wr_slot])
        _gla_mix(xb_ref, gain_ref, wout_ref, o_ref, proj_ref.at[rd_slot], bcum_ref.at[rd_slot],
                 og_ref, s_ref, first, pieces, heads=heads, dk=dk, dv=dv)

    @pl.when(even)
    def _():
        step(0, 1)

    @pl.when(jnp.logical_not(even))
    def _():
        step(1, 0)


def _gla_prep(norm_g, w_in, w_gate_up, b_gate, head_gain, w_out):
    n, d, _ = w_in.shape
    _, rank, kd = w_gate_up.shape
    w_in_pad = jnp.pad(w_in.astype(BF16), ((0, 0), (0, 0), (0, LANES - rank)))
    w_up_pad = jnp.pad(w_gate_up.astype(BF16), ((0, 0), (0, LANES - rank), (0, 0)))
    return (norm_g.reshape(n, 1, d), w_in_pad, w_up_pad, b_gate.reshape(n, 1, kd),
            head_gain.reshape(n, 1, -1), w_out.astype(BF16))


def _gla_layer(x2d, bsz, seqlen, prep, layer):
    _, _, w_up_pad, _, _, w_out = prep
    t, d = x2d.shape
    kd = w_up_pad.shape[2]
    d_inner = w_out.shape[1]
    dk = kd // GLA_HEADS
    dv = d_inner // GLA_HEADS
    n_proj = 2 * kd + 2 * d_inner
    nt = seqlen // GLA_TILE
    n_tiles = bsz * nt
    return pl.pallas_call(
        functools.partial(_gla_layer_kernel, nt=nt, heads=GLA_HEADS, dk=dk, dv=dv),
        grid=(n_tiles + 1,),
        in_specs=[pl.BlockSpec((GLA_TILE, d), lambda s: (jnp.minimum(s, n_tiles - 1), 0)),
                  pl.BlockSpec((GLA_TILE, d), lambda s: (jnp.maximum(s - 1, 0), 0))]
                 + [_layer_resident(p, layer) for p in prep],
        out_specs=pl.BlockSpec((GLA_TILE, d), lambda s: (jnp.maximum(s - 1, 0), 0)),
        out_shape=jax.ShapeDtypeStruct((t, d), F32),
        scratch_shapes=[pltpu.VMEM((GLA_TILE, d), BF16),
                        pltpu.VMEM((GLA_TILE, LANES), BF16),
                        pltpu.VMEM((3, GLA_TILE, kd), BF16),
                        pltpu.VMEM((2, GLA_TILE, n_proj), F32),
                        pltpu.VMEM((2, GLA_TILE, kd), F32),
                        pltpu.VMEM((GLA_TILE, d_inner), BF16),
                        pltpu.VMEM((GLA_HEADS, dk, dv), F32)],
        compiler_params=_params("arbitrary"),
        name="gla_layer",
    )(x2d, x2d, *prep)


def _ssm_expand(bc_ref, cc_ref, dc_ref, bm_ref, cm_ref, cb_ref):
    gi = dc_ref.shape[2]
    ns = bc_ref.shape[3] // 2
    gstates = ns // (LANES // gi)
    bc = bc_ref[0, 0].astype(BF16)
    cc = cc_ref[0, 0].astype(BF16)
    dc = dc_ref[0, 0].astype(BF16)

    r, c = _iota2((2 * LANES, 2 * gi))
    sel = jnp.where((r // LANES == c // gi) & (r % gi == c % gi), 1.0, 0.0).astype(BF16)
    r, c = _iota2((2 * LANES, 2 * ns))
    keep = (r % LANES) // gi == (c % ns) // gstates
    bm_ref[...] = jnp.where(keep, _mm(sel, bc), 0.0).astype(BF16)

    r, c = _iota2((2 * gi, 2 * LANES))
    sel_t = jnp.where((c // LANES == r // gi) & (c % gi == r % gi), 1.0, 0.0).astype(BF16)
    r, c = _iota2((2 * ns, 2 * LANES))
    keep = (r % ns) // gstates == (c % LANES) // gi
    cm_ref[...] = jnp.where(keep, _mm_tn(cc, sel_t), 0.0).astype(BF16)

    r, c = _iota2((gi, LANES))
    sel_o = jnp.where(c % gi == r, 1.0, 0.0).astype(BF16)
    r, c = _iota2((LANES, LANES))
    cb_ref[...] = jnp.where(r // gi == c // gi, _mm_tn(dc, sel_o), 0.0).astype(BF16)


def _ssm_kernel(u_ref, bc_ref, cc_ref, dc_ref, a2re_ref, a2im_ref, d_ref, y_ref,
                bm_ref, cm_ref, cb_ref, u2_ref, w_ref, tb_ref, st_ref, zsh_ref, *, nb):
    lanes = u_ref.shape[1]
    npair = u_ref.shape[0] // (2 * nb)
    rows = npair * nb
    ns = st_ref.shape[1] // 2
    sub = SSM_SUB_PAIRS
    sub_rows = sub * nb

    @pl.when(pl.program_id(1) == 0)
    def _():
        st_ref[...] = jnp.zeros_like(st_ref)
        zsh_ref[0:nb, :] = jnp.zeros((nb, lanes), F32)
        _ssm_expand(bc_ref, cc_ref, dc_ref, bm_ref, cm_ref, cb_ref)

    def even_odd(k):
        u = u_ref[2 * k * sub_rows:2 * (k + 1) * sub_rows, :].reshape(sub, 2 * nb, lanes)
        return u[:, 0:nb, :].reshape(sub_rows, lanes), u[:, nb:2 * nb, :].reshape(sub_rows, lanes)

    for k in range(npair // sub):
        ue, uo = even_odd(k)
        u2_ref[k * sub_rows:(k + 1) * sub_rows, :] = jnp.concatenate([ue, uo], axis=1).astype(BF16)

    a_re = jnp.broadcast_to(a2re_ref[0, 0], (nb, ns))
    a_im = jnp.broadcast_to(a2im_ref[0, 0], (nb, ns))
    d = d_ref[0]
    s_re = st_ref[:, 0:ns]
    s_im = st_ref[:, ns:2 * ns]

    def drive(k):
        blk = slice(k * sub_rows, (k + 1) * sub_rows)
        w_ref[blk, :] = _mm(u2_ref[blk, :], bm_ref[...])

    def readout(k):
        blk = slice(k * sub_rows, (k + 1) * sub_rows)
        z = _mm(tb_ref[blk, :], cm_ref[...])
        zsh_ref[nb + k * sub_rows:nb + (k + 1) * sub_rows, :] = z[:, 0:lanes]
        ue, uo = even_odd(k)
        ye = zsh_ref[blk, :] + _mm(u2_ref[blk, 0:lanes], cb_ref[...]) + d * ue
        yo = z[:, lanes:2 * lanes] + d * uo
        y_ref[2 * k * sub_rows:2 * (k + 1) * sub_rows, :] = jnp.concatenate(
            [ye.reshape(sub, nb, lanes), yo.reshape(sub, nb, lanes)], axis=1).reshape(2 * sub_rows, lanes)

    n_sub = npair // sub
    drive(0)
    for k in range(n_sub):
        if k + 1 < n_sub:
            drive(k + 1)
        if k >= 1:
            readout(k - 1)
        for m in range(0, sub, 2):
            r0 = k * sub_rows + m * nb
            m_re = a_re * s_re - a_im * s_im + w_ref[r0:r0 + nb, 0:ns]
            m_im = a_re * s_im + a_im * s_re + w_ref[r0:r0 + nb, ns:2 * ns]
            s_re = a_re * m_re - a_im * m_im + w_ref[r0 + nb:r0 + 2 * nb, 0:ns]
            s_im = a_re * m_im + a_im * m_re + w_ref[r0 + nb:r0 + 2 * nb, ns:2 * ns]
            tb_ref[r0:r0 + 2 * nb, 0:ns] = jnp.concatenate([m_re, s_re], axis=0).astype(BF16)
            tb_ref[r0:r0 + 2 * nb, ns:2 * ns] = jnp.concatenate([m_im, s_im], axis=0).astype(BF16)
    readout(n_sub - 1)

    st_ref[:, 0:ns] = s_re
    st_ref[:, ns:2 * ns] = s_im
    zsh_ref[0:nb, :] = zsh_ref[rows:rows + nb, :]


def _ssm_core(proj, prep, layer, bsz):
    bcomp, ccomp, dcomp, a2_re, a2_im, d_row = prep
    _, nblk, gi2, two_ns = bcomp.shape
    d_inner = nblk * LANES
    tl = SSM_TILE
    rows = (tl // 2) * bsz
    per_block = lambda *blk: pl.BlockSpec((1, 1) + blk, lambda j, i: (layer, j, 0, 0))
    return pl.pallas_call(
        functools.partial(_ssm_kernel, nb=bsz),
        grid=(nblk, proj.shape[0] // (tl * bsz)),
        in_specs=[
            pl.BlockSpec((tl * bsz, LANES), lambda j, i: (i, j)),
            per_block(gi2, two_ns),
            per_block(gi2, two_ns),
            per_block(gi2 // 2, LANES),
            per_block(1, two_ns // 2),
            per_block(1, two_ns // 2),
            pl.BlockSpec((1, 1, LANES), lambda j, i: (layer, 0, j)),
        ],
        out_specs=pl.BlockSpec((tl * bsz, LANES), lambda j, i: (i, j)),
        out_shape=jax.ShapeDtypeStruct((proj.shape[0], d_inner), F32),
        scratch_shapes=[pltpu.VMEM((2 * LANES, two_ns), BF16),
                        pltpu.VMEM((two_ns, 2 * LANES), BF16),
                        pltpu.VMEM((LANES, LANES), BF16),
                        pltpu.VMEM((rows, 2 * LANES), BF16),
                        pltpu.VMEM((rows, two_ns), F32),
                        pltpu.VMEM((rows, two_ns), BF16),
                        pltpu.VMEM((bsz, two_ns), F32),
                        pltpu.VMEM((rows + bsz, LANES), F32)],
        compiler_params=_params("parallel", "arbitrary"),
        name="ssm_core",
    )(proj, bcomp, ccomp, dcomp, a2_re, a2_im, d_row)


def _glu_out_kernel(y_ref, z_ref, wg_ref, bg_ref, wo_ref, x_ref, *rest):
    o_ref, slab_ref = rest[-2:]
    nb, tl, d = x_ref.shape
    y = jax.nn.gelu(y_ref[...])
    y = y * jax.nn.sigmoid(_mm(y.astype(BF16), wg_ref[...]) + bg_ref[...])
    y = y * jax.nn.silu(z_ref[...])
    out = _to_time_major(x_ref, slab_ref) + _mm(y.astype(BF16), wo_ref[...])
    if len(rest) == 3:
        out = _rms(out, rest[0][...])
    for c in range(d // LANES):
        slab_ref[c] = out[:, c * LANES:(c + 1) * LANES]
    for b in range(nb):
        for c in range(d // LANES):
            o_ref[b, :, c * LANES:(c + 1) * LANES] = slab_ref[c, pl.ds(b, tl, stride=nb), :]


def _glu_out(y2d, proj, w_glu, b_glu, w_out, layer, x3d, final_gain=None):
    t, di = y2d.shape
    bsz, seqlen, d = x3d.shape
    tl = ROW_TILE // bsz
    in_specs = [pl.BlockSpec((ROW_TILE, di), lambda i: (i, 0)),
                pl.BlockSpec((ROW_TILE, di), lambda i: (i, 1)),
                _layer_resident(w_glu, layer),
                _layer_resident(b_glu, layer),
                _layer_resident(w_out, layer),
                pl.BlockSpec((bsz, tl, d), lambda i: (0, i, 0))]
    args = [y2d, proj, w_glu, b_glu, w_out, x3d]
    if final_gain is not None:
        in_specs.append(_resident((1, d)))
        args.append(final_gain)
    return pl.pallas_call(
        _glu_out_kernel,
        grid=(t // ROW_TILE,),
        in_specs=in_specs,
        out_specs=pl.BlockSpec((bsz, tl, d), lambda i: (0, i, 0)),
        out_shape=jax.ShapeDtypeStruct((bsz, seqlen, d), F32),
        scratch_shapes=[pltpu.VMEM((d // LANES, ROW_TILE, LANES), F32)],
        compiler_params=_params("parallel"),
        name="glu_out",
    )(*args)


def _s5_prep(lam_re, lam_im, log_dt, b_re, b_im, c_re, c_im, d_skip):
    n, g, p = lam_re.shape
    i = b_re.shape[-1]
    gl = LANES // i
    nblk = g // gl
    ns = gl * p
    dt = jnp.exp(log_dt)[..., None]
    mag = jnp.exp(lam_re * dt)
    a_re = mag * jnp.cos(lam_im * dt)
    a_im = mag * jnp.sin(lam_im * dt)
    den = lam_re * lam_re + lam_im * lam_im
    bc_re = ((a_re - 1.0) * lam_re + a_im * lam_im) / den
    bc_im = (a_im * lam_re - (a_re - 1.0) * lam_im) / den
    bb_re = bc_re[..., None] * b_re - bc_im[..., None] * b_im
    bb_im = bc_re[..., None] * b_im + bc_im[..., None] * b_re
    ab_re = a_re[..., None] * bb_re - a_im[..., None] * bb_im
    ab_im = a_re[..., None] * bb_im + a_im[..., None] * bb_re
    ca_re = c_re * a_re[:, :, None, :] - c_im * a_im[:, :, None, :]
    ca_im = c_re * a_im[:, :, None, :] + c_im * a_re[:, :, None, :]
    cb = (jnp.einsum('ngop,ngpi->ngoi', c_re, bb_re, precision=lax.Precision.HIGHEST)
          - jnp.einsum('ngop,ngpi->ngoi', c_im, bb_im, precision=lax.Precision.HIGHEST))

    bsrc = jnp.stack([ab_re, ab_im, bb_re, bb_im], axis=1).reshape(n, 2, 2, nblk, gl, p, i)
    bcomp = jnp.transpose(bsrc, (0, 3, 1, 6, 2, 4, 5)).reshape(n, nblk, 2 * i, 2 * ns)
    csrc = jnp.stack([ca_re, -ca_im, c_re, -c_im], axis=1).reshape(n, 2, 2, nblk, gl, i, p)
    ccomp = jnp.transpose(csrc, (0, 3, 1, 5, 2, 4, 6)).reshape(n, nblk, 2 * i, 2 * ns)
    dcomp = jnp.transpose(cb.reshape(n, nblk, gl, i, i), (0, 1, 3, 2, 4)).reshape(n, nblk, i, gl * i)
    a2_re = (a_re * a_re - a_im * a_im).reshape(n, nblk, 1, ns)
    a2_im = (2.0 * a_re * a_im).reshape(n, nblk, 1, ns)
    return bcomp, ccomp, dcomp, a2_re, a2_im, d_skip.reshape(n, 1, g * i)


def _s5_layer(x2d, bsz, seqlen, dense, prep, layer, final_gain):
    norm_g, w_in, w_glu, b_glu, w_out = dense
    x3d = x2d.reshape(bsz, seqlen, x2d.shape[1])
    proj = _norm_proj(x3d, norm_g, w_in, layer)
    y = _ssm_core(proj, prep, layer, bsz)
    out = _glu_out(y, proj, w_glu, b_glu, w_out, layer, x3d, final_gain)
    return out.reshape(x2d.shape)


def kernel(x, gla_norm, gla_w_in, gla_w_gate_up, gla_b_gate, gla_head_gain, gla_w_out,
           s5_norm, s5_w_in, s5_lam_re, s5_lam_im, s5_log_dt, s5_b_re, s5_b_im,
           s5_c_re, s5_c_im, s5_d, s5_w_glu, s5_b_glu, s5_w_out, final_norm):
    bsz, seqlen, d = x.shape
    n_s5 = s5_w_in.shape[0]
    depth = gla_w_in.shape[0] + n_s5
    gla_prep = _gla_prep(gla_norm, gla_w_in, gla_w_gate_up, gla_b_gate, gla_head_gain, gla_w_out)
    s5_dense = (s5_norm.reshape(n_s5, 1, d), s5_w_in.astype(BF16), s5_w_glu.astype(BF16),
                s5_b_glu.reshape(n_s5, 1, -1), s5_w_out.astype(BF16))
    s5_prep = _s5_prep(s5_lam_re, s5_lam_im, s5_log_dt, s5_b_re, s5_b_im, s5_c_re, s5_c_im, s5_d)
    final_gain = final_norm.reshape(1, d)
    x2d = x.reshape(bsz * seqlen, d)
    for i in range(depth):
        j = i // 2
        if i % 2 == 0:
            x2d = _gla_layer(x2d, bsz, seqlen, gla_prep, j)
        else:
            x2d = _s5_layer(x2d, bsz, seqlen, s5_dense, s5_prep, j,
                            final_gain if i == depth - 1 else None)
    if depth % 2 == 1:
        x2d = _final_norm(x2d, final_gain)
    return x2d.reshape(bsz, seqlen, d)
```

```python
import functools

import jax
import jax.numpy as jnp
from jax import lax
from jax.experimental import pallas as pl
from jax.experimental.pallas import tpu as pltpu

F32 = jnp.float32
BF16 = jnp.bfloat16

EPS = 1e-6
GLA_HEADS = 4
GLA_GATE_TAU = 16.0
GLA_CHUNK = 64

LANES = 128
VMEM_LIMIT_BYTES = 56 * 1024 * 1024

ROW_TILE = 512
GLA_TILE = 256
PROJ_SLAB = 256
SSM_TILE = 1024
SSM_SUB_PAIRS = 64


def _params(*sem):
    return pltpu.CompilerParams(dimension_semantics=sem, vmem_limit_bytes=VMEM_LIMIT_BYTES)


def _mm(a, b):
    return jnp.dot(a, b, preferred_element_type=F32)


def _mm_nt(a, b):
    return lax.dot_general(a, b, (((1,), (1,)), ((), ())), preferred_element_type=F32)


def _mm_tn(a, b):
    return lax.dot_general(a, b, (((0,), (0,)), ((), ())), preferred_element_type=F32)


def _iota2(shape):
    return (lax.broadcasted_iota(jnp.int32, shape, 0), lax.broadcasted_iota(jnp.int32, shape, 1))


def _rms(x, g):
    return x * lax.rsqrt(jnp.mean(x * x, axis=-1, keepdims=True) + EPS) * g


def _resident(shape):
    return pl.BlockSpec(shape, lambda *_: (0,) * len(shape), pipeline_mode=pl.Buffered(1))


def _layer_resident(stacked, layer):
    nd = stacked.ndim
    return pl.BlockSpec((None,) + stacked.shape[1:], lambda *_: (layer,) + (0,) * (nd - 1),
                        pipeline_mode=pl.Buffered(1))


def _to_time_major(x_ref, slab_ref):
    nb, tl, d = x_ref.shape
    for b in range(nb):
        for c in range(d // LANES):
            slab_ref[c, pl.ds(b, tl, stride=nb), :] = x_ref[b, :, c * LANES:(c + 1) * LANES]
    return jnp.concatenate([slab_ref[c] for c in range(d // LANES)], axis=1)


def _norm_proj_kernel(x_ref, g_ref, w_ref, o_ref, slab_ref):
    h = _rms(_to_time_major(x_ref, slab_ref), g_ref[...])
    o_ref[...] = _mm(h.astype(BF16), w_ref[...])


def _norm_proj(x3d, gains, weights, layer):
    bsz, seqlen, d = x3d.shape
    n = weights.shape[2]
    tl = ROW_TILE // bsz
    return pl.pallas_call(
        _norm_proj_kernel,
        grid=(seqlen // tl,),
        in_specs=[pl.BlockSpec((bsz, tl, d), lambda i: (0, i, 0)),
                  _layer_resident(gains, layer),
                  _layer_resident(weights, layer)],
        out_specs=pl.BlockSpec((ROW_TILE, n), lambda i: (i, 0)),
        out_shape=jax.ShapeDtypeStruct((seqlen * bsz, n), F32),
        scratch_shapes=[pltpu.VMEM((d // LANES, ROW_TILE, LANES), F32)],
        compiler_params=_params("parallel"),
        name="norm_proj",
    )(x3d, gains, weights)


def _final_norm_kernel(x_ref, g_ref, o_ref):
    o_ref[...] = _rms(x_ref[...], g_ref[...])


def _final_norm(x2d, g):
    t, d = x2d.shape
    return pl.pallas_call(
        _final_norm_kernel,
        grid=(t // ROW_TILE,),
        in_specs=[pl.BlockSpec((ROW_TILE, d), lambda i: (i, 0)),
                  pl.BlockSpec((1, d), lambda i: (0, 0))],
        out_specs=pl.BlockSpec((ROW_TILE, d), lambda i: (i, 0)),
        out_shape=jax.ShapeDtypeStruct((t, d), F32),
        compiler_params=_params("parallel"),
        name="final_norm",
    )(x2d, g)


def _gla_project_pieces(x_ref, g_ref, win_ref, wup_ref, bg_ref, h_ref, r_ref, la_ref, proj_ref, bcum_ref):
    c = GLA_CHUNK

    def norm():
        h_ref[...] = _rms(x_ref[...], g_ref[...]).astype(BF16)

    def slab(j):
        cols = slice(j * PROJ_SLAB, (j + 1) * PROJ_SLAB)

        def run():
            proj_ref[:, cols] = _mm(h_ref[...], win_ref[:, cols])
        return run

    def gate_in():
        n_main = proj_ref.shape[-1]
        r_ref[...] = _mm(h_ref[...], win_ref[:, n_main:n_main + LANES]).astype(BF16)

    def log_decay():
        la_ref[...] = jax.nn.log_sigmoid(_mm(r_ref[...], wup_ref[...]) + bg_ref[...]) / GLA_GATE_TAU

    def cumsum():
        x = la_ref[...]
        pos = lax.broadcasted_iota(jnp.int32, x.shape, 0) % c
        shift = 1
        while shift < c:
            x = x + jnp.where(pos >= shift, pltpu.roll(x, shift, 0), 0.0)
            shift *= 2
        bcum_ref[...] = x

    slabs = [slab(j) for j in range(proj_ref.shape[-1] // PROJ_SLAB)]
    third = len(slabs) // 3
    return [norm, gate_in] + slabs[:third] + [log_decay] + slabs[third:2 * third] + [cumsum] + slabs[2 * third:]


def _gla_mix(x_ref, gain_ref, wout_ref, o_ref, proj_ref, bcum_ref, og_ref, s_ref, first, other,
             *, heads, dk, dv):
    tl = x_ref.shape[0]
    c = GLA_CHUNK
    kd = heads * dk
    di = heads * dv
    q0, k0, v0, z0 = 0, kd, 2 * kd, 2 * kd + di

    c2 = 2 * c
    prow, pcol = _iota2((c, c2))
    top_keep = pcol <= prow
    bot_keep = (pcol >= c) & (pcol - c <= prow)
    first_rows = lax.broadcasted_iota(jnp.int32, (c2, dk), 0) < c
    drow, dcol = _iota2((dk, dk))
    eye = drow == dcol
    scale = dk ** -0.5
    gain = gain_ref[...]

    n_units = (tl // c2) * heads
    n_slots = 2 * n_units

    def emit_other(slot):
        for piece in other[1 + slot * (len(other) - 1) // n_slots:
                           1 + (slot + 1) * (len(other) - 1) // n_slots]:
            piece()

    def scores(unit):
        cp, hh = divmod(unit, heads)
        sl = slice(cp * c2, (cp + 1) * c2)
        bc = bcum_ref[sl, hh * dk:(hh + 1) * dk]
        bl1 = bc[c - 1:c, :]
        bl2 = bc[c2 - 1:c2, :]
        dec1 = jnp.exp(bl1)
        dec2 = jnp.exp(bl2)
        q = proj_ref[sl, q0 + hh * dk:q0 + (hh + 1) * dk]
        k = proj_ref[sl, k0 + hh * dk:k0 + (hh + 1) * dk]
        qg = (q * scale) * jnp.exp(bc)
        qgb = qg.astype(BF16)
        kgb = (k * jnp.exp(-bc)).astype(BF16)
        ke = k * jnp.exp(jnp.where(first_rows, bl1, bl2) - bc)
        ke1z = jnp.where(first_rows, ke, 0.0).astype(BF16)
        s_all = _mm_nt(qgb, kgb)
        att = jnp.concatenate(
            [jnp.where(top_keep, s_all[0:c], 0.0),
             jnp.where(bot_keep, s_all[c:c2], 0.0) + _mm_nt(qgb[c:c2], ke1z)], axis=0).astype(BF16)
        lhs_q = jnp.where(first_rows, qg, qg * dec1).astype(BF16)
        kpb = jnp.where(first_rows, ke * dec2, ke).astype(BF16)
        return lhs_q, kpb, att, bl1 + bl2

    def outputs(unit, lhs_q, kpb, att, bl12):
        cp, hh = divmod(unit, heads)
        sl = slice(cp * c2, (cp + 1) * c2)
        vb = proj_ref[sl, v0 + hh * dv:v0 + (hh + 1) * dv].astype(BF16)
        s_prev = s_ref[hh]
        if cp == 0:
            s_prev = jnp.where(first, 0.0, s_prev)
        o = _mm(jnp.concatenate([lhs_q, att], axis=1),
                jnp.concatenate([s_prev.astype(BF16), vb], axis=0))

        dec = jnp.exp(bl12)
        dec_col = jnp.sum(jnp.where(eye, jnp.broadcast_to(dec, (dk, dk)), 0.0),
                          axis=1, keepdims=True)
        s_ref[hh] = dec_col * s_prev + _mm_tn(kpb, vb)

        z = proj_ref[sl, z0 + hh * dv:z0 + (hh + 1) * dv]
        og_ref[sl, hh * dv:(hh + 1) * dv] = (_rms(o, gain) * jax.nn.silu(z)).astype(BF16)

    other[0]()
    ready = scores(0)
    for unit in range(n_units):
        emit_other(2 * unit)
        nxt = scores(unit + 1) if unit + 1 < n_units else None
        emit_other(2 * unit + 1)
        outputs(unit, *ready)
        ready = nxt

    o_ref[...] = x_ref[...] + _mm(og_ref[...], wout_ref[...])


def _gla_layer_kernel(xa_ref, xb_ref, g_ref, win_ref, wup_ref, bg_ref, gain_ref, wout_ref,
                      o_ref, h_ref, r_ref, la_ref, proj_ref, bcum_ref, og_ref, s_ref, *, nt, heads, dk, dv):
    s = pl.program_id(0)

    @pl.when(s == 0)
    def _():
        proj_ref[1] = jnp.zeros(proj_ref.shape[1:], F32)
        bcum_ref[1] = jnp.zeros(bcum_ref.shape[1:], F32)
        s_ref[...] = jnp.zeros_like(s_ref)

    first = lax.rem(s + (nt - 1), nt) == 0
    even = lax.rem(s, 2) == 0

    def step(wr_slot, rd_slot):
        pieces = _gla_project_pieces(xa_ref, g_ref, win_ref, wup_ref, bg_ref, h_ref, r_ref, la_ref,
                                     proj_ref.at[wr_slot], bcum_ref.at[wr_slot])
        _gla_mix(xb_ref, gain_ref, wout_ref, o_ref, proj_ref.at[rd_slot], bcum_ref.at[rd_slot],
                 og_ref, s_ref, first, pieces, heads=heads, dk=dk, dv=dv)

    @pl.when(even)
    def _():
        step(0, 1)

    @pl.when(jnp.logical_not(even))
    def _():
        step(1, 0)


def _gla_prep(norm_g, w_in, w_gate_up, b_gate, head_gain, w_out):
    n, d, _ = w_in.shape
    _, rank, kd = w_gate_up.shape
    w_in_pad = jnp.pad(w_in.astype(BF16), ((0, 0), (0, 0), (0, LANES - rank)))
    w_up_pad = jnp.pad(w_gate_up.astype(BF16), ((0, 0), (0, LANES - rank), (0, 0)))
    return (norm_g.reshape(n, 1, d), w_in_pad, w_up_pad, b_gate.reshape(n, 1, kd),
            head_gain.reshape(n, 1, -1), w_out.astype(BF16))


def _gla_layer(x2d, bsz, seqlen, prep, layer):
    _, _, w_up_pad, _, _, w_out = prep
    t, d = x2d.shape
    kd = w_up_pad.shape[2]
    d_inner = w_out.shape[1]
    dk = kd // GLA_HEADS
    dv = d_inner // GLA_HEADS
    n_proj = 2 * kd + 2 * d_inner
    nt = seqlen // GLA_TILE
    n_tiles = bsz * nt
    return pl.pallas_call(
        functools.partial(_gla_layer_kernel, nt=nt, heads=GLA_HEADS, dk=dk, dv=dv),
        grid=(n_tiles + 1,),
        in_specs=[pl.BlockSpec((GLA_TILE, d), lambda s: (jnp.minimum(s, n_tiles - 1), 0)),
                  pl.BlockSpec((GLA_TILE, d), lambda s: (jnp.maximum(s - 1, 0), 0))]
                 + [_layer_resident(p, layer) for p in prep],
        out_specs=pl.BlockSpec((GLA_TILE, d), lambda s: (jnp.maximum(s - 1, 0), 0)),
        out_shape=jax.ShapeDtypeStruct((t, d), F32),
        scratch_shapes=[pltpu.VMEM((GLA_TILE, d), BF16),
                        pltpu.VMEM((GLA_TILE, LANES), BF16),
                        pltpu.VMEM((GLA_TILE, kd), F32),
                        pltpu.VMEM((2, GLA_TILE, n_proj), F32),
                        pltpu.VMEM((2, GLA_TILE, kd), F32),
                        pltpu.VMEM((GLA_TILE, d_inner), BF16),
                        pltpu.VMEM((GLA_HEADS, dk, dv), F32)],
        compiler_params=_params("arbitrary"),
        name="gla_layer",
    )(x2d, x2d, *prep)


def _ssm_expand(bc_ref, cc_ref, dc_ref, bm_ref, cm_ref, cb_ref):
    gi = dc_ref.shape[2]
    ns = bc_ref.shape[3] // 2
    gstates = ns // (LANES // gi)
    bc = bc_ref[0, 0].astype(BF16)
    cc = cc_ref[0, 0].astype(BF16)
    dc = dc_ref[0, 0].astype(BF16)

    r, c = _iota2((2 * LANES, 2 * gi))
    sel = jnp.where((r // LANES == c // gi) & (r % gi == c % gi), 1.0, 0.0).astype(BF16)
    r, c = _iota2((2 * LANES, 2 * ns))
    keep = (r % LANES) // gi == (c % ns) // gstates
    bm_ref[...] = jnp.where(keep, _mm(sel, bc), 0.0).astype(BF16)

    r, c = _iota2((2 * gi, 2 * LANES))
    sel_t = jnp.where((c // LANES == r // gi) & (c % gi == r % gi), 1.0, 0.0).astype(BF16)
    r, c = _iota2((2 * ns, 2 * LANES))
    keep = (r % ns) // gstates == (c % LANES) // gi
    cm_ref[...] = jnp.where(keep, _mm_tn(cc, sel_t), 0.0).astype(BF16)

    r, c = _iota2((gi, LANES))
    sel_o = jnp.where(c % gi == r, 1.0, 0.0).astype(BF16)
    r, c = _iota2((LANES, LANES))
    cb_ref[...] = jnp.where(r // gi == c // gi, _mm_tn(dc, sel_o), 0.0).astype(BF16)


def _ssm_kernel(u_ref, bc_ref, cc_ref, dc_ref, a2re_ref, a2im_ref, d_ref, y_ref,
                bm_ref, cm_ref, cb_ref, u2_ref, w_ref, tb_ref, st_ref, zsh_ref, *, nb):
    lanes = u_ref.shape[1]
    npair = u_ref.shape[0] // (2 * nb)
    rows = npair * nb
    ns = st_ref.shape[1] // 2
    sub = SSM_SUB_PAIRS
    sub_rows = sub * nb

    @pl.when(pl.program_id(1) == 0)
    def _():
        st_ref[...] = jnp.zeros_like(st_ref)
        zsh_ref[0:nb, :] = jnp.zeros((nb, lanes), F32)
        _ssm_expand(bc_ref, cc_ref, dc_ref, bm_ref, cm_ref, cb_ref)

    def even_odd(k):
        u = u_ref[2 * k * sub_rows:2 * (k + 1) * sub_rows, :].reshape(sub, 2 * nb, lanes)
        return u[:, 0:nb, :].reshape(sub_rows, lanes), u[:, nb:2 * nb, :].reshape(sub_rows, lanes)

    for k in range(npair // sub):
        ue, uo = even_odd(k)
        u2_ref[k * sub_rows:(k + 1) * sub_rows, :] = jnp.concatenate([ue, uo], axis=1).astype(BF16)

    a_re = jnp.broadcast_to(a2re_ref[0, 0], (nb, ns))
    a_im = jnp.broadcast_to(a2im_ref[0, 0], (nb, ns))
    d = d_ref[0]
    s_re = st_ref[:, 0:ns]
    s_im = st_ref[:, ns:2 * ns]

    def drive(k):
        blk = slice(k * sub_rows, (k + 1) * sub_rows)
        w_ref[blk, :] = _mm(u2_ref[blk, :], bm_ref[...])

    def readout(k):
        blk = slice(k * sub_rows, (k + 1) * sub_rows)
        z = _mm(tb_ref[blk, :], cm_ref[...])
        zsh_ref[nb + k * sub_rows:nb + (k + 1) * sub_rows, :] = z[:, 0:lanes]
        ue, uo = even_odd(k)
        ye = zsh_ref[blk, :] + _mm(u2_ref[blk, 0:lanes], cb_ref[...]) + d * ue
        yo = z[:, lanes:2 * lanes] + d * uo
        y_ref[2 * k * sub_rows:2 * (k + 1) * sub_rows, :] = jnp.concatenate(
            [ye.reshape(sub, nb, lanes), yo.reshape(sub, nb, lanes)], axis=1).reshape(2 * sub_rows, lanes)

    n_sub = npair // sub
    drive(0)
    for k in range(n_sub):
        if k + 1 < n_sub:
            drive(k + 1)
        if k >= 1:
            readout(k - 1)
        for m in range(0, sub, 2):
            r0 = k * sub_rows + m * nb
            m_re = a_re * s_re - a_im * s_im + w_ref[r0:r0 + nb, 0:ns]
            m_im = a_re * s_im + a_im * s_re + w_ref[r0:r0 + nb, ns:2 * ns]
            s_re = a_re * m_re - a_im * m_im + w_ref[r0 + nb:r0 + 2 * nb, 0:ns]
            s_im = a_re * m_im + a_im * m_re + w_ref[r0 + nb:r0 + 2 * nb, ns:2 * ns]
            tb_ref[r0:r0 + 2 * nb, 0:ns] = jnp.concatenate([m_re, s_re], axis=0).astype(BF16)
            tb_ref[r0:r0 + 2 * nb, ns:2 * ns] = jnp.concatenate([m_im, s_im], axis=0).astype(BF16)
    readout(n_sub - 1)

    st_ref[:, 0:ns] = s_re
    st_ref[:, ns:2 * ns] = s_im
    zsh_ref[0:nb, :] = zsh_ref[rows:rows + nb, :]


def _ssm_core(proj, prep, layer, bsz):
    bcomp, ccomp, dcomp, a2_re, a2_im, d_row = prep
    _, nblk, gi2, two_ns = bcomp.shape
    d_inner = nblk * LANES
    tl = SSM_TILE
    rows = (tl // 2) * bsz
    per_block = lambda *blk: pl.BlockSpec((1, 1) + blk, lambda j, i: (layer, j, 0, 0))
    return pl.pallas_call(
        functools.partial(_ssm_kernel, nb=bsz),
        grid=(nblk, proj.shape[0] // (tl * bsz)),
        in_specs=[
            pl.BlockSpec((tl * bsz, LANES), lambda j, i: (i, j)),
            per_block(gi2, two_ns),
            per_block(gi2, two_ns),
            per_block(gi2 // 2, LANES),
            per_block(1, two_ns // 2),
            per_block(1, two_ns // 2),
            pl.BlockSpec((1, 1, LANES), lambda j, i: (layer, 0, j)),
        ],
        out_specs=pl.BlockSpec((tl * bsz, LANES), lambda j, i: (i, j)),
        out_shape=jax.ShapeDtypeStruct((proj.shape[0], d_inner), F32),
        scratch_shapes=[pltpu.VMEM((2 * LANES, two_ns), BF16),
                        pltpu.VMEM((two_ns, 2 * LANES), BF16),
                        pltpu.VMEM((LANES, LANES), BF16),
                        pltpu.VMEM((rows, 2 * LANES), BF16),
                        pltpu.VMEM((rows, two_ns), F32),
                        pltpu.VMEM((rows, two_ns), BF16),
                        pltpu.VMEM((bsz, two_ns), F32),
                        pltpu.VMEM((rows + bsz, LANES), F32)],
        compiler_params=_params("parallel", "arbitrary"),
        name="ssm_core",
    )(proj, bcomp, ccomp, dcomp, a2_re, a2_im, d_row)


def _glu_out_kernel(y_ref, z_ref, wg_ref, bg_ref, wo_ref, x_ref, *rest):
    o_ref, slab_ref = rest[-2:]
    nb, tl, d = x_ref.shape
    y = jax.nn.gelu(y_ref[...])
    y = y * jax.nn.sigmoid(_mm(y.astype(BF16), wg_ref[...]) + bg_ref[...])
    y = y * jax.nn.silu(z_ref[...])
    out = _to_time_major(x_ref, slab_ref) + _mm(y.astype(BF16), wo_ref[...])
    if len(rest) == 3:
        out = _rms(out, rest[0][...])
    for c in range(d // LANES):
        slab_ref[c] = out[:, c * LANES:(c + 1) * LANES]
    for b in range(nb):
        for c in range(d // LANES):
            o_ref[b, :, c * LANES:(c + 1) * LANES] = slab_ref[c, pl.ds(b, tl, stride=nb), :]


def _glu_out(y2d, proj, w_glu, b_glu, w_out, layer, x3d, final_gain=None):
    t, di = y2d.shape
    bsz, seqlen, d = x3d.shape
    tl = ROW_TILE // bsz
    in_specs = [pl.BlockSpec((ROW_TILE, di), lambda i: (i, 0)),
                pl.BlockSpec((ROW_TILE, di), lambda i: (i, 1)),
                _layer_resident(w_glu, layer),
                _layer_resident(b_glu, layer),
                _layer_resident(w_out, layer),
                pl.BlockSpec((bsz, tl, d), lambda i: (0, i, 0))]
    args = [y2d, proj, w_glu, b_glu, w_out, x3d]
    if final_gain is not None:
        in_specs.append(_resident((1, d)))
        args.append(final_gain)
    return pl.pallas_call(
        _glu_out_kernel,
        grid=(t // ROW_TILE,),
        in_specs=in_specs,
        out_specs=pl.BlockSpec((bsz, tl, d), lambda i: (0, i, 0)),
        out_shape=jax.ShapeDtypeStruct((bsz, seqlen, d), F32),
        scratch_shapes=[pltpu.VMEM((d // LANES, ROW_TILE, LANES), F32)],
        compiler_params=_params("parallel"),
        name="glu_out",
    )(*args)


def _s5_prep(lam_re, lam_im, log_dt, b_re, b_im, c_re, c_im, d_skip):
    n, g, p = lam_re.shape
    i = b_re.shape[-1]
    gl = LANES // i
    nblk = g // gl
    ns = gl * p
    dt = jnp.exp(log_dt)[..., None]
    mag = jnp.exp(lam_re * dt)
    a_re = mag * jnp.cos(lam_im * dt)
    a_im = mag * jnp.sin(lam_im * dt)
    den = lam_re * lam_re + lam_im * lam_im
    bc_re = ((a_re - 1.0) * lam_re + a_im * lam_im) / den
    bc_im = (a_im * lam_re - (a_re - 1.0) * lam_im) / den
    bb_re = bc_re[..., None] * b_re - bc_im[..., None] * b_im
    bb_im = bc_re[..., None] * b_im + bc_im[..., None] * b_re
    ab_re = a_re[..., None] * bb_re - a_im[..., None] * bb_im
    ab_im = a_re[..., None] * bb_im + a_im[..., None] * bb_re
    ca_re = c_re * a_re[:, :, None, :] - c_im * a_im[:, :, None, :]
    ca_im = c_re * a_im[:, :, None, :] + c_im * a_re[:, :, None, :]
    cb = (jnp.einsum('ngop,ngpi->ngoi', c_re, bb_re, precision=lax.Precision.HIGHEST)
          - jnp.einsum('ngop,ngpi->ngoi', c_im, bb_im, precision=lax.Precision.HIGHEST))

    bsrc = jnp.stack([ab_re, ab_im, bb_re, bb_im], axis=1).reshape(n, 2, 2, nblk, gl, p, i)
    bcomp = jnp.transpose(bsrc, (0, 3, 1, 6, 2, 4, 5)).reshape(n, nblk, 2 * i, 2 * ns)
    csrc = jnp.stack([ca_re, -ca_im, c_re, -c_im], axis=1).reshape(n, 2, 2, nblk, gl, i, p)
    ccomp = jnp.transpose(csrc, (0, 3, 1, 5, 2, 4, 6)).reshape(n, nblk, 2 * i, 2 * ns)
    dcomp = jnp.transpose(cb.reshape(n, nblk, gl, i, i), (0, 1, 3, 2, 4)).reshape(n, nblk, i, gl * i)
    a2_re = (a_re * a_re - a_im * a_im).reshape(n, nblk, 1, ns)
    a2_im = (2.0 * a_re * a_im).reshape(n, nblk, 1, ns)
    return bcomp, ccomp, dcomp, a2_re, a2_im, d_skip.reshape(n, 1, g * i)


def _s5_layer(x2d, bsz, seqlen, dense, prep, layer, final_gain):
    norm_g, w_in, w_glu, b_glu, w_out = dense
    x3d = x2d.reshape(bsz, seqlen, x2d.shape[1])
    proj = _norm_proj(x3d, norm_g, w_in, layer)
    y = _ssm_core(proj, prep, layer, bsz)
    out = _glu_out(y, proj, w_glu, b_glu, w_out, layer, x3d, final_gain)
    return out.reshape(x2d.shape)


def kernel(x, gla_norm, gla_w_in, gla_w_gate_up, gla_b_gate, gla_head_gain, gla_w_out,
           s5_norm, s5_w_in, s5_lam_re, s5_lam_im, s5_log_dt, s5_b_re, s5_b_im,
           s5_c_re, s5_c_im, s5_d, s5_w_glu, s5_b_glu, s5_w_out, final_norm):
    bsz, seqlen, d = x.shape
    n_s5 = s5_w_in.shape[0]
    depth = gla_w_in.shape[0] + n_s5
    gla_prep = _gla_prep(gla_norm, gla_w_in, gla_w_gate_up, gla_b_gate, gla_head_gain, gla_w_out)
    s5_dense = (s5_norm.reshape(n_s5, 1, d), s5_w_in.astype(BF16), s5_w_glu.astype(BF16),
                s5_b_glu.reshape(n_s5, 1, -1), s5_w_out.astype(BF16))
    s5_prep = _s5_prep(s5_lam_re, s5_lam_im, s5_log_dt, s5_b_re, s5_b_im, s5_c_re, s5_c_im, s5_d)
    final_gain = final_norm.reshape(1, d)
    x2d = x.reshape(bsz * seqlen, d)
    for i in range(depth):
        j = i // 2
        if i % 2 == 0:
            x2d = _gla_layer(x2d, bsz, seqlen, gla_prep, j)
        else:
            x2d = _s5_layer(x2d, bsz, seqlen, s5_dense, s5_prep, j,
                            final_gain if i == depth - 1 else None)
    if depth % 2 == 1:
        x2d = _final_norm(x2d, final_gain)
    return x2d.reshape(bsz, seqlen, d)
```

```python
import functools

import jax
import jax.numpy as jnp
from jax import lax
from jax.experimental import pallas as pl
from jax.experimental.pallas import tpu as pltpu

F32 = jnp.float32
BF16 = jnp.bfloat16

EPS = 1e-6
GLA_HEADS = 4
GLA_GATE_TAU = 16.0
GLA_CHUNK = 64

LANES = 128
VMEM_LIMIT_BYTES = 56 * 1024 * 1024

ROW_TILE = 512
GLA_TILE = 256
PROJ_SLAB = 256
SSM_TILE = 1024
SSM_SUB_PAIRS = 64


def _params(*sem):
    return pltpu.CompilerParams(dimension_semantics=sem, vmem_limit_bytes=VMEM_LIMIT_BYTES)


def _mm(a, b):
    return jnp.dot(a, b, preferred_element_type=F32)


def _mm_nt(a, b):
    return lax.dot_general(a, b, (((1,), (1,)), ((), ())), preferred_element_type=F32)


def _mm_tn(a, b):
    return lax.dot_general(a, b, (((0,), (0,)), ((), ())), preferred_element_type=F32)


def _iota2(shape):
    return (lax.broadcasted_iota(jnp.int32, shape, 0), lax.broadcasted_iota(jnp.int32, shape, 1))


def _rms(x, g):
    return x * lax.rsqrt(jnp.mean(x * x, axis=-1, keepdims=True) + EPS) * g


def _resident(shape):
    return pl.BlockSpec(shape, lambda *_: (0,) * len(shape), pipeline_mode=pl.Buffered(1))


def _layer_resident(stacked, layer):
    nd = stacked.ndim
    return pl.BlockSpec((None,) + stacked.shape[1:], lambda *_: (layer,) + (0,) * (nd - 1),
                        pipeline_mode=pl.Buffered(1))


def _to_time_major(x_ref, slab_ref):
    nb, tl, d = x_ref.shape
    for b in range(nb):
        for c in range(d // LANES):
            slab_ref[c, pl.ds(b, tl, stride=nb), :] = x_ref[b, :, c * LANES:(c + 1) * LANES]
    return jnp.concatenate([slab_ref[c] for c in range(d // LANES)], axis=1)


def _norm_proj_kernel(x_ref, g_ref, w_ref, o_ref, slab_ref):
    h = _rms(_to_time_major(x_ref, slab_ref), g_ref[...])
    o_ref[...] = _mm(h.astype(BF16), w_ref[...])


def _norm_proj(x3d, gains, weights, layer):
    bsz, seqlen, d = x3d.shape
    n = weights.shape[2]
    tl = ROW_TILE // bsz
    return pl.pallas_call(
        _norm_proj_kernel,
        grid=(seqlen // tl,),
        in_specs=[pl.BlockSpec((bsz, tl, d), lambda i: (0, i, 0)),
                  _layer_resident(gains, layer),
                  _layer_resident(weights, layer)],
        out_specs=pl.BlockSpec((ROW_TILE, n), lambda i: (i, 0)),
        out_shape=jax.ShapeDtypeStruct((seqlen * bsz, n), F32),
        scratch_shapes=[pltpu.VMEM((d // LANES, ROW_TILE, LANES), F32)],
        compiler_params=_params("parallel"),
        name="norm_proj",
    )(x3d, gains, weights)


def _final_norm_kernel(x_ref, g_ref, o_ref):
    o_ref[...] = _rms(x_ref[...], g_ref[...])


def _final_norm(x2d, g):
    t, d = x2d.shape
    return pl.pallas_call(
        _final_norm_kernel,
        grid=(t // ROW_TILE,),
        in_specs=[pl.BlockSpec((ROW_TILE, d), lambda i: (i, 0)),
                  pl.BlockSpec((1, d), lambda i: (0, 0))],
        out_specs=pl.BlockSpec((ROW_TILE, d), lambda i: (i, 0)),
        out_shape=jax.ShapeDtypeStruct((t, d), F32),
        compiler_params=_params("parallel"),
        name="final_norm",
    )(x2d, g)


def _gla_project_pieces(x_ref, g_ref, win_ref, wrt_ref, wup_ref, bg_ref, h_ref, rt_ref, la_ref, proj_ref,
                        bcum_ref):
    c = GLA_CHUNK

    def norm():
        h_ref[...] = _rms(x_ref[...], g_ref[...]).astype(BF16)

    def slab(j):
        cols = slice(j * PROJ_SLAB, (j + 1) * PROJ_SLAB)

        def run():
            proj_ref[:, cols] = _mm(h_ref[...], win_ref[:, cols])
        return run

    def gate_in():
        rt_ref[...] = _mm_nt(wrt_ref[...], h_ref[...]).astype(BF16)

    def log_decay():
        la_ref[...] = jax.nn.log_sigmoid(_mm_tn(rt_ref[...], wup_ref[...]) + bg_ref[...]) / GLA_GATE_TAU

    def cumsum():
        x = la_ref[...]
        pos = lax.broadcasted_iota(jnp.int32, x.shape, 0) % c
        shift = 1
        while shift < c:
            x = x + jnp.where(pos >= shift, pltpu.roll(x, shift, 0), 0.0)
            shift *= 2
        bcum_ref[...] = x

    slabs = [slab(j) for j in range(proj_ref.shape[-1] // PROJ_SLAB)]
    third = len(slabs) // 3
    return [norm, gate_in] + slabs[:third] + [log_decay] + slabs[third:2 * third] + [cumsum] + slabs[2 * third:]


def _gla_mix(x_ref, gain_ref, wout_ref, o_ref, proj_ref, bcum_ref, og_ref, s_ref, first, other,
             *, heads, dk, dv):
    tl = x_ref.shape[0]
    c = GLA_CHUNK
    kd = heads * dk
    di = heads * dv
    q0, k0, v0, z0 = 0, kd, 2 * kd, 2 * kd + di

    c2 = 2 * c
    prow, pcol = _iota2((c, c2))
    top_keep = pcol <= prow
    bot_keep = (pcol >= c) & (pcol - c <= prow)
    first_rows = lax.broadcasted_iota(jnp.int32, (c2, dk), 0) < c
    drow, dcol = _iota2((dk, dk))
    eye = drow == dcol
    scale = dk ** -0.5
    gain = gain_ref[...]

    n_units = (tl // c2) * heads
    n_slots = 2 * n_units

    def emit_other(slot):
        for piece in other[1 + slot * (len(other) - 1) // n_slots:
                           1 + (slot + 1) * (len(other) - 1) // n_slots]:
            piece()

    def scores(unit):
        cp, hh = divmod(unit, heads)
        sl = slice(cp * c2, (cp + 1) * c2)
        bc = bcum_ref[sl, hh * dk:(hh + 1) * dk]
        bl1 = bc[c - 1:c, :]
        bl2 = bc[c2 - 1:c2, :]
        dec1 = jnp.exp(bl1)
        dec2 = jnp.exp(bl2)
        q = proj_ref[sl, q0 + hh * dk:q0 + (hh + 1) * dk]
        k = proj_ref[sl, k0 + hh * dk:k0 + (hh + 1) * dk]
        qg = (q * scale) * jnp.exp(bc)
        qgb = qg.astype(BF16)
        kgb = (k * jnp.exp(-bc)).astype(BF16)
        ke = k * jnp.exp(jnp.where(first_rows, bl1, bl2) - bc)
        ke1z = jnp.where(first_rows, ke, 0.0).astype(BF16)
        s_all = _mm_nt(qgb, jnp.concatenate([kgb, ke1z], axis=0))
        att = jnp.concatenate(
            [jnp.where(top_keep, s_all[0:c, 0:c2], 0.0),
             jnp.where(bot_keep, s_all[c:c2, 0:c2], 0.0) + s_all[c:c2, c2:2 * c2]], axis=0).astype(BF16)
        lhs_q = jnp.where(first_rows, qg, qg * dec1).astype(BF16)
        kpb = jnp.where(first_rows, ke * dec2, ke).astype(BF16)
        return lhs_q, kpb, att, bl1 + bl2

    def outputs(unit, lhs_q, kpb, att, bl12):
        cp, hh = divmod(unit, heads)
        sl = slice(cp * c2, (cp + 1) * c2)
        vb = proj_ref[sl, v0 + hh * dv:v0 + (hh + 1) * dv].astype(BF16)
        s_prev = s_ref[hh]
        if cp == 0:
            s_prev = jnp.where(first, 0.0, s_prev)
        o = _mm(jnp.concatenate([lhs_q, att], axis=1),
                jnp.concatenate([s_prev.astype(BF16), vb], axis=0))

        dec = jnp.exp(bl12)
        dec_col = jnp.sum(jnp.where(eye, jnp.broadcast_to(dec, (dk, dk)), 0.0),
                          axis=1, keepdims=True)
        s_ref[hh] = dec_col * s_prev + _mm_tn(kpb, vb)

        z = proj_ref[sl, z0 + hh * dv:z0 + (hh + 1) * dv]
        og_ref[sl, hh * dv:(hh + 1) * dv] = (_rms(o, gain) * jax.nn.silu(z)).astype(BF16)

    other[0]()
    ready = scores(0)
    for unit in range(n_units):
        emit_other(2 * unit)
        nxt = scores(unit + 1) if unit + 1 < n_units else None
        emit_other(2 * unit + 1)
        outputs(unit, *ready)
        ready = nxt

    o_ref[...] = x_ref[...] + _mm(og_ref[...], wout_ref[...])


def _gla_layer_kernel(xa_ref, xb_ref, g_ref, win_ref, wrt_ref, wup_ref, bg_ref, gain_ref, wout_ref,
                      o_ref, h_ref, rt_ref, la_ref, proj_ref, bcum_ref, og_ref, s_ref, *, nt, heads, dk, dv):
    s = pl.program_id(0)

    @pl.when(s == 0)
    def _():
        proj_ref[1] = jnp.zeros(proj_ref.shape[1:], F32)
        bcum_ref[1] = jnp.zeros(bcum_ref.shape[1:], F32)
        s_ref[...] = jnp.zeros_like(s_ref)

    first = lax.rem(s + (nt - 1), nt) == 0
    even = lax.rem(s, 2) == 0

    def step(wr_slot, rd_slot):
        pieces = _gla_project_pieces(xa_ref, g_ref, win_ref, wrt_ref, wup_ref, bg_ref, h_ref, rt_ref, la_ref,
                                     proj_ref.at[wr_slot], bcum_ref.at[wr_slot])
        _gla_mix(xb_ref, gain_ref, wout_ref, o_ref, proj_ref.at[rd_slot], bcum_ref.at[rd_slot],
                 og_ref, s_ref, first, pieces, heads=heads, dk=dk, dv=dv)

    @pl.when(even)
    def _():
        step(0, 1)

    @pl.when(jnp.logical_not(even))
    def _():
        step(1, 0)


def _gla_prep(norm_g, w_in, w_gate_up, b_gate, head_gain, w_out):
    n, d, n_in = w_in.shape
    _, rank, kd = w_gate_up.shape
    w_in_pad = jnp.pad(w_in.astype(BF16), ((0, 0), (0, 0), (0, LANES - rank)))
    w_r_t = jnp.swapaxes(w_in[:, :, n_in - rank:], 1, 2).astype(BF16)
    return (norm_g.reshape(n, 1, d), w_in_pad, w_r_t, w_gate_up.astype(BF16), b_gate.reshape(n, 1, kd),
            head_gain.reshape(n, 1, -1), w_out.astype(BF16))


def _gla_layer(x2d, bsz, seqlen, prep, layer):
    _, _, _, w_up, _, _, w_out = prep
    t, d = x2d.shape
    rank, kd = w_up.shape[1:]
    d_inner = w_out.shape[1]
    dk = kd // GLA_HEADS
    dv = d_inner // GLA_HEADS
    n_proj = 2 * kd + 2 * d_inner
    nt = seqlen // GLA_TILE
    n_tiles = bsz * nt
    return pl.pallas_call(
        functools.partial(_gla_layer_kernel, nt=nt, heads=GLA_HEADS, dk=dk, dv=dv),
        grid=(n_tiles + 1,),
        in_specs=[pl.BlockSpec((GLA_TILE, d), lambda s: (jnp.minimum(s, n_tiles - 1), 0)),
                  pl.BlockSpec((GLA_TILE, d), lambda s: (jnp.maximum(s - 1, 0), 0))]
                 + [_layer_resident(p, layer) for p in prep],
        out_specs=pl.BlockSpec((GLA_TILE, d), lambda s: (jnp.maximum(s - 1, 0), 0)),
        out_shape=jax.ShapeDtypeStruct((t, d), F32),
        scratch_shapes=[pltpu.VMEM((GLA_TILE, d), BF16),
                        pltpu.VMEM((rank, GLA_TILE), BF16),
                        pltpu.VMEM((GLA_TILE, kd), F32),
                        pltpu.VMEM((2, GLA_TILE, n_proj), F32),
                        pltpu.VMEM((2, GLA_TILE, kd), F32),
                        pltpu.VMEM((GLA_TILE, d_inner), BF16),
                        pltpu.VMEM((GLA_HEADS, dk, dv), F32)],
        compiler_params=_params("arbitrary"),
        name="gla_layer",
    )(x2d, x2d, *prep)


def _ssm_expand(bc_ref, cc_ref, dc_ref, bm_ref, cm_ref, cb_ref):
    gi = dc_ref.shape[2]
    ns = bc_ref.shape[3] // 2
    gstates = ns // (LANES // gi)
    bc = bc_ref[0, 0].astype(BF16)
    cc = cc_ref[0, 0].astype(BF16)
    dc = dc_ref[0, 0].astype(BF16)

    r, c = _iota2((2 * LANES, 2 * gi))
    sel = jnp.where((r // LANES == c // gi) & (r % gi == c % gi), 1.0, 0.0).astype(BF16)
    r, c = _iota2((2 * LANES, 2 * ns))
    keep = (r % LANES) // gi == (c % ns) // gstates
    bm_ref[...] = jnp.where(keep, _mm(sel, bc), 0.0).astype(BF16)

    r, c = _iota2((2 * gi, 2 * LANES))
    sel_t = jnp.where((c // LANES == r // gi) & (c % gi == r % gi), 1.0, 0.0).astype(BF16)
    r, c = _iota2((2 * ns, 2 * LANES))
    keep = (r % ns) // gstates == (c % LANES) // gi
    cm_ref[...] = jnp.where(keep, _mm_tn(cc, sel_t), 0.0).astype(BF16)

    r, c = _iota2((gi, LANES))
    sel_o = jnp.where(c % gi == r, 1.0, 0.0).astype(BF16)
    r, c = _iota2((LANES, LANES))
    cb_ref[...] = jnp.where(r // gi == c // gi, _mm_tn(dc, sel_o), 0.0).astype(BF16)


def _ssm_kernel(u_ref, bc_ref, cc_ref, dc_ref, a2re_ref, a2im_ref, d_ref, y_ref,
                bm_ref, cm_ref, cb_ref, u2_ref, w_ref, tb_ref, st_ref, zsh_ref, *, nb):
    lanes = u_ref.shape[1]
    npair = u_ref.shape[0] // (2 * nb)
    rows = npair * nb
    ns = st_ref.shape[1] // 2
    sub = SSM_SUB_PAIRS
    sub_rows = sub * nb

    @pl.when(pl.program_id(1) == 0)
    def _():
        st_ref[...] = jnp.zeros_like(st_ref)
        zsh_ref[0:nb, :] = jnp.zeros((nb, lanes), F32)
        _ssm_expand(bc_ref, cc_ref, dc_ref, bm_ref, cm_ref, cb_ref)

    def even_odd(k):
        u = u_ref[2 * k * sub_rows:2 * (k + 1) * sub_rows, :].reshape(sub, 2 * nb, lanes)
        return u[:, 0:nb, :].reshape(sub_rows, lanes), u[:, nb:2 * nb, :].reshape(sub_rows, lanes)

    for k in range(npair // sub):
        ue, uo = even_odd(k)
        u2_ref[k * sub_rows:(k + 1) * sub_rows, :] = jnp.concatenate([ue, uo], axis=1).astype(BF16)

    a_re = jnp.broadcast_to(a2re_ref[0, 0], (nb, ns))
    a_im = jnp.broadcast_to(a2im_ref[0, 0], (nb, ns))
    d = d_ref[0]
    s_re = st_ref[:, 0:ns]
    s_im = st_ref[:, ns:2 * ns]

    def drive(k):
        blk = slice(k * sub_rows, (k + 1) * sub_rows)
        w_ref[blk, :] = _mm(u2_ref[blk, :], bm_ref[...])

    def readout(k):
        blk = slice(k * sub_rows, (k + 1) * sub_rows)
        z = _mm(tb_ref[blk, :], cm_ref[...])
        zsh_ref[nb + k * sub_rows:nb + (k + 1) * sub_rows, :] = z[:, 0:lanes]
        ue, uo = even_odd(k)
        ye = zsh_ref[blk, :] + _mm(u2_ref[blk, 0:lanes], cb_ref[...]) + d * ue
        yo = z[:, lanes:2 * lanes] + d * uo
        y_ref[2 * k * sub_rows:2 * (k + 1) * sub_rows, :] = jnp.concatenate(
            [ye.reshape(sub, nb, lanes), yo.reshape(sub, nb, lanes)], axis=1).reshape(2 * sub_rows, lanes)

    n_sub = npair // sub
    drive(0)
    for k in range(n_sub):
        if k + 1 < n_sub:
            drive(k + 1)
        if k >= 1:
            readout(k - 1)
        for m in range(0, sub, 2):
            r0 = k * sub_rows + m * nb
            m_re = a_re * s_re - a_im * s_im + w_ref[r0:r0 + nb, 0:ns]
            m_im = a_re * s_im + a_im * s_re + w_ref[r0:r0 + nb, ns:2 * ns]
            s_re = a_re * m_re - a_im * m_im + w_ref[r0 + nb:r0 + 2 * nb, 0:ns]
            s_im = a_re * m_im + a_im * m_re + w_ref[r0 + nb:r0 + 2 * nb, ns:2 * ns]
            tb_ref[r0:r0 + 2 * nb, 0:ns] = jnp.concatenate([m_re, s_re], axis=0).astype(BF16)
            tb_ref[r0:r0 + 2 * nb, ns:2 * ns] = jnp.concatenate([m_im, s_im], axis=0).astype(BF16)
    readout(n_sub - 1)

    st_ref[:, 0:ns] = s_re
    st_ref[:, ns:2 * ns] = s_im
    zsh_ref[0:nb, :] = zsh_ref[rows:rows + nb, :]


def _ssm_core(proj, prep, layer, bsz):
    bcomp, ccomp, dcomp, a2_re, a2_im, d_row = prep
    _, nblk, gi2, two_ns = bcomp.shape
    d_inner = nblk * LANES
    tl = SSM_TILE
    rows = (tl // 2) * bsz
    per_block = lambda *blk: pl.BlockSpec((1, 1) + blk, lambda j, i: (layer, j, 0, 0))
    return pl.pallas_call(
        functools.partial(_ssm_kernel, nb=bsz),
        grid=(nblk, proj.shape[0] // (tl * bsz)),
        in_specs=[
            pl.BlockSpec((tl * bsz, LANES), lambda j, i: (i, j)),
            per_block(gi2, two_ns),
            per_block(gi2, two_ns),
            per_block(gi2 // 2, LANES),
            per_block(1, two_ns // 2),
            per_block(1, two_ns // 2),
            pl.BlockSpec((1, 1, LANES), lambda j, i: (layer, 0, j)),
        ],
        out_specs=pl.BlockSpec((tl * bsz, LANES), lambda j, i: (i, j)),
        out_shape=jax.ShapeDtypeStruct((proj.shape[0], d_inner), F32),
        scratch_shapes=[pltpu.VMEM((2 * LANES, two_ns), BF16),
                        pltpu.VMEM((two_ns, 2 * LANES), BF16),
                        pltpu.VMEM((LANES, LANES), BF16),
                        pltpu.VMEM((rows, 2 * LANES), BF16),
                        pltpu.VMEM((rows, two_ns), F32),
                        pltpu.VMEM((rows, two_ns), BF16),
                        pltpu.VMEM((bsz, two_ns), F32),
                        pltpu.VMEM((rows + bsz, LANES), F32)],
        compiler_params=_params("parallel", "arbitrary"),
        name="ssm_core",
    )(proj, bcomp, ccomp, dcomp, a2_re, a2_im, d_row)


def _glu_out_kernel(y_ref, z_ref, wg_ref, bg_ref, wo_ref, x_ref, *rest):
    o_ref, slab_ref = rest[-2:]
    nb, tl, d = x_ref.shape
    y = jax.nn.gelu(y_ref[...])
    y = y * jax.nn.sigmoid(_mm(y.astype(BF16), wg_ref[...]) + bg_ref[...])
    y = y * jax.nn.silu(z_ref[...])
    out = _to_time_major(x_ref, slab_ref) + _mm(y.astype(BF16), wo_ref[...])
    if len(rest) == 3:
        out = _rms(out, rest[0][...])
    for c in range(d // LANES):
        slab_ref[c] = out[:, c * LANES:(c + 1) * LANES]
    for b in range(nb):
        for c in range(d // LANES):
            o_ref[b, :, c * LANES:(c + 1) * LANES] = slab_ref[c, pl.ds(b, tl, stride=nb), :]


def _glu_out(y2d, proj, w_glu, b_glu, w_out, layer, x3d, final_gain=None):
    t, di = y2d.shape
    bsz, seqlen, d = x3d.shape
    tl = ROW_TILE // bsz
    in_specs = [pl.BlockSpec((ROW_TILE, di), lambda i: (i, 0)),
                pl.BlockSpec((ROW_TILE, di), lambda i: (i, 1)),
                _layer_resident(w_glu, layer),
                _layer_resident(b_glu, layer),
                _layer_resident(w_out, layer),
                pl.BlockSpec((bsz, tl, d), lambda i: (0, i, 0))]
    args = [y2d, proj, w_glu, b_glu, w_out, x3d]
    if final_gain is not None:
        in_specs.append(_resident((1, d)))
        args.append(final_gain)
    return pl.pallas_call(
        _glu_out_kernel,
        grid=(t // ROW_TILE,),
        in_specs=in_specs,
        out_specs=pl.BlockSpec((bsz, tl, d), lambda i: (0, i, 0)),
        out_shape=jax.ShapeDtypeStruct((bsz, seqlen, d), F32),
        scratch_shapes=[pltpu.VMEM((d // LANES, ROW_TILE, LANES), F32)],
        compiler_params=_params("parallel"),
        name="glu_out",
    )(*args)


def _s5_prep(lam_re, lam_im, log_dt, b_re, b_im, c_re, c_im, d_skip):
    n, g, p = lam_re.shape
    i = b_re.shape[-1]
    gl = LANES // i
    nblk = g // gl
    ns = gl * p
    dt = jnp.exp(log_dt)[..., None]
    mag = jnp.exp(lam_re * dt)
    a_re = mag * jnp.cos(lam_im * dt)
    a_im = mag * jnp.sin(lam_im * dt)
    den = lam_re * lam_re + lam_im * lam_im
    bc_re = ((a_re - 1.0) * lam_re + a_im * lam_im) / den
    bc_im = (a_im * lam_re - (a_re - 1.0) * lam_im) / den
    bb_re = bc_re[..., None] * b_re - bc_im[..., None] * b_im
    bb_im = bc_re[..., None] * b_im + bc_im[..., None] * b_re
    ab_re = a_re[..., None] * bb_re - a_im[..., None] * bb_im
    ab_im = a_re[..., None] * bb_im + a_im[..., None] * bb_re
    ca_re = c_re * a_re[:, :, None, :] - c_im * a_im[:, :, None, :]
    ca_im = c_re * a_im[:, :, None, :] + c_im * a_re[:, :, None, :]
    cb = (jnp.einsum('ngop,ngpi->ngoi', c_re, bb_re, precision=lax.Precision.HIGHEST)
          - jnp.einsum('ngop,ngpi->ngoi', c_im, bb_im, precision=lax.Precision.HIGHEST))

    bsrc = jnp.stack([ab_re, ab_im, bb_re, bb_im], axis=1).reshape(n, 2, 2, nblk, gl, p, i)
    bcomp = jnp.transpose(bsrc, (0, 3, 1, 6, 2, 4, 5)).reshape(n, nblk, 2 * i, 2 * ns)
    csrc = jnp.stack([ca_re, -ca_im, c_re, -c_im], axis=1).reshape(n, 2, 2, nblk, gl, i, p)
    ccomp = jnp.transpose(csrc, (0, 3, 1, 5, 2, 4, 6)).reshape(n, nblk, 2 * i, 2 * ns)
    dcomp = jnp.transpose(cb.reshape(n, nblk, gl, i, i), (0, 1, 3, 2, 4)).reshape(n, nblk, i, gl * i)
    a2_re = (a_re * a_re - a_im * a_im).reshape(n, nblk, 1, ns)
    a2_im = (2.0 * a_re * a_im).reshape(n, nblk, 1, ns)
    return bcomp, ccomp, dcomp, a2_re, a2_im, d_skip.reshape(n, 1, g * i)


def _s5_layer(x2d, bsz, seqlen, dense, prep, layer, final_gain):
    norm_g, w_in, w_glu, b_glu, w_out = dense
    x3d = x2d.reshape(bsz, seqlen, x2d.shape[1])
    proj = _norm_proj(x3d, norm_g, w_in, layer)
    y = _ssm_core(proj, prep, layer, bsz)
    out = _glu_out(y, proj, w_glu, b_glu, w_out, layer, x3d, final_gain)
    return out.reshape(x2d.shape)


def kernel(x, gla_norm, gla_w_in, gla_w_gate_up, gla_b_gate, gla_head_gain, gla_w_out,
           s5_norm, s5_w_in, s5_lam_re, s5_lam_im, s5_log_dt, s5_b_re, s5_b_im,
           s5_c_re, s5_c_im, s5_d, s5_w_glu, s5_b_glu, s5_w_out, final_norm):
    bsz, seqlen, d = x.shape
    n_s5 = s5_w_in.shape[0]
    depth = gla_w_in.shape[0] + n_s5
    gla_prep = _gla_prep(gla_norm, gla_w_in, gla_w_gate_up, gla_b_gate, gla_head_gain, gla_w_out)
    s5_dense = (s5_norm.reshape(n_s5, 1, d), s5_w_in.astype(BF16), s5_w_glu.astype(BF16),
                s5_b_glu.reshape(n_s5, 1, -1), s5_w_out.astype(BF16))
    s5_prep = _s5_prep(s5_lam_re, s5_lam_im, s5_log_dt, s5_b_re, s5_b_im, s5_c_re, s5_c_im, s5_d)
    final_gain = final_norm.reshape(1, d)
    x2d = x.reshape(bsz * seqlen, d)
    for i in range(depth):
        j = i // 2
        if i % 2 == 0:
            x2d = _gla_layer(x2d, bsz, seqlen, gla_prep, j)
        else:
            x2d = _s5_layer(x2d, bsz, seqlen, s5_dense, s5_prep, j,
                            final_gain if i == depth - 1 else None)
    if depth % 2 == 1:
        x2d = _final_norm(x2d, final_gain)
    return x2d.reshape(bsz, seqlen, d)
```
